```python
import jax, jax.numpy as jnp
from jax import lax
import numpy as np

D_MODEL = 4096
BATCH = 4
SEQ = 4096
DEPTH = 4
DEC_BATCH = 8
DEC_SEQ = 64
PAST_LEN = 2048

CHUNK = 64
MIX = D_MODEL
D_POOL = MIX // 2
D_SGU = MIX - D_POOL
POOL_WINDOWS = (2, 4, 8, 16)
POOL_GROUPS = len(POOL_WINDOWS)
POOL_GC = D_POOL // POOL_GROUPS
POOL_CACHE = max(POOL_WINDOWS) - 1
SGU_CHUNK = 128
SGU_HEADS = 8
SGU_HD = D_SGU // SGU_HEADS
D_PROJ = D_POOL + 2 * D_SGU
MOE_GROUPS = 4
MOE_EXP_PER_GROUP = 4
N_EXPERTS = MOE_GROUPS * MOE_EXP_PER_GROUP
TOP_K = 2
D_EXPERT = 1024
MOE_BLOCK = 128
PLE_DIM = 256
EPS = 1e-6

kernel_name = "hymba_pool_sgu_hmoe_stream_step"


def rms_norm(x, g):
    xf = x.astype(jnp.float32)
    y = xf * lax.rsqrt(jnp.mean(xf * xf, axis=-1, keepdims=True) + EPS) * g.astype(jnp.float32)
    return y.astype(x.dtype)


def layer_norm(x, g, b):
    xf = x.astype(jnp.float32)
    mu = jnp.mean(xf, axis=-1, keepdims=True)
    var = jnp.mean(jnp.square(xf - mu), axis=-1, keepdims=True)
    y = (xf - mu) * lax.rsqrt(var + EPS) * g.astype(jnp.float32) + b.astype(jnp.float32)
    return y.astype(x.dtype)


def pool_mix(za, past, n_past, w_pool, scale):
    S = za.shape[1]
    P = past.shape[1]
    ext = jnp.concatenate([past.astype(za.dtype), za], axis=1)
    extf = ext.astype(jnp.float32)
    cs = jnp.concatenate([jnp.zeros_like(extf[:, :1]), jnp.cumsum(extf, axis=1)], axis=1)
    pos = jnp.arange(S)
    outs = []
    for g, w in enumerate(POOL_WINDOWS):
        sl = slice(g * POOL_GC, (g + 1) * POOL_GC)
        win_sum = cs[:, P + 1:P + 1 + S, sl] - cs[:, P + 1 - w:P + 1 - w + S, sl]
        cnt = jnp.minimum(w, pos + 1 + n_past).astype(jnp.float32)
        d = win_sum / cnt[None, :, None] - extf[:, P:, sl]
        outs.append(jnp.einsum('bsc,cd->bsd', d, w_pool[g].astype(jnp.float32)))
    y = jnp.concatenate(outs, axis=-1) * scale.astype(jnp.float32)
    new_past = ext[:, -P:]
    return y.astype(za.dtype), new_past


def sgu(zb, ln_g, ln_b, w_s, b_s):
    B, S, _ = zb.shape
    u, v = zb[..., :D_SGU], zb[..., D_SGU:]
    v = layer_norm(v, ln_g, ln_b)
    L = min(S, SGU_CHUNK)
    n = S // L
    vh = v.reshape(B, n, L, SGU_HEADS, SGU_HD)
    wm = jnp.tril(w_s[:, :L, :L])
    mixed = jnp.einsum('hij,bnjhc->bnihc', wm, vh) + b_s[:, :L].T[:, :, None]
    return u * mixed.reshape(B, S, D_SGU), v


def hmoe(h, rg_w, rg_b, re_w, re_b, w1, w3, w2):
    lead = h.shape[:-1]
    hf = h.reshape(-1, D_MODEL)
    T = hf.shape[0]
    lg = (hf @ rg_w).astype(jnp.float32) + rg_b.astype(jnp.float32)
    pg = jax.nn.softmax(lg, axis=-1)
    grp = jnp.argmax(lg, axis=-1)
    le = ((hf @ re_w).astype(jnp.float32) + re_b.astype(jnp.float32)).reshape(T, MOE_GROUPS, MOE_EXP_PER_GROUP)
    le_g = jnp.take_along_axis(le, grp[:, None, None], axis=1)[:, 0]
    top_v, top_i = lax.top_k(le_g, TOP_K)
    gate = jnp.take_along_axis(pg, grp[:, None], axis=1) * jax.nn.softmax(top_v, axis=-1)
    eid = grp[:, None] * MOE_EXP_PER_GROUP + top_i

    M = T * TOP_K
    a_e = eid.reshape(M)
    a_t = jnp.repeat(jnp.arange(T), TOP_K)
    a_w = gate.reshape(M)
    order = jnp.argsort(a_e)
    se, st, sw = a_e[order], a_t[order], a_w[order]
    counts = jnp.bincount(a_e, length=N_EXPERTS)
    padded = (counts + MOE_BLOCK - 1) // MOE_BLOCK * MOE_BLOCK
    start = jnp.cumsum(counts) - counts
    pend = jnp.cumsum(padded)
    pstart = pend - padded
    dest = pstart[se] + (jnp.arange(M) - start[se])
    R = (M + N_EXPERTS * (MOE_BLOCK - 1) + MOE_BLOCK - 1) // MOE_BLOCK * MOE_BLOCK
    nb = R // MOE_BLOCK
    xbuf = jnp.zeros((R, D_MODEL), hf.dtype).at[dest].set(hf[st])
    blk_exp = jnp.minimum(jnp.searchsorted(pend, jnp.arange(nb) * MOE_BLOCK, side='right'), N_EXPERTS - 1)

    def expert_block(args):
        xb, e = args
        return (jax.nn.silu(xb @ w1[e]) * (xb @ w3[e])) @ w2[e]

    ybuf = lax.map(expert_block, (xbuf.reshape(nb, MOE_BLOCK, D_MODEL), blk_exp)).reshape(R, D_MODEL)
    y = ybuf[dest].astype(jnp.float32) * sw[:, None]
    out = jax.ops.segment_sum(y, st, num_segments=T)
    return out.astype(h.dtype).reshape(*lead, D_MODEL)


def block(x, p, pool_past, n_past, g_mix, w_in, w_pool, pool_scale, sgu_ln_g, sgu_ln_b, sgu_w, sgu_b,
          w_out, g_ffn, rg_w, rg_b, re_w, re_b, w1, w3, w2, g_ple, ple_gd, ple_gu, ple_proj):
    h = rms_norm(x, g_mix)
    z = h @ w_in
    za = z[..., :D_POOL]
    zb = jax.nn.gelu(z[..., D_POOL:], approximate=False)
    a_out, new_pool = pool_mix(za, pool_past, n_past, w_pool, pool_scale)
    b_out, v = sgu(zb, sgu_ln_g, sgu_ln_b, sgu_w, sgu_b)
    x = x + jnp.concatenate([a_out, b_out], axis=-1) @ w_out
    x = x + hmoe(rms_norm(x, g_ffn), rg_w, rg_b, re_w, re_b, w1, w3, w2)
    gate = jax.nn.sigmoid((rms_norm(x, g_ple) @ ple_gd) @ ple_gu)
    x = x + gate * (p @ ple_proj)
    return x, new_pool, v


def setup_inputs(seed: int = 0) -> dict:
    key = jax.random.key(seed)
    ks = jax.random.split(key, 32)
    f32 = jnp.float32

    def nrm(k, shape, s):
        return jax.random.normal(k, shape, f32) * s

    def gain(k, shape):
        return 1.0 + 0.05 * jax.random.normal(k, shape, f32)

    return {
        "x_prompt": nrm(ks[0], (BATCH, SEQ, D_MODEL), 1.0),
        "x_sample": nrm(ks[1], (DEC_BATCH, DEC_SEQ, D_MODEL), 1.0),
        "state_pool": nrm(ks[2], (DEPTH, DEC_BATCH, POOL_CACHE, D_POOL), 1.0),
        "p_prompt": nrm(ks[3], (DEPTH, BATCH, SEQ, PLE_DIM), 1.0),
        "p_sample": nrm(ks[4], (DEPTH, DEC_BATCH, DEC_SEQ, PLE_DIM), 1.0),
        "g_mix": gain(ks[5], (DEPTH, D_MODEL)),
        "w_in": nrm(ks[6], (DEPTH, D_MODEL, D_PROJ), D_MODEL ** -0.5),
        "w_pool": nrm(ks[7], (DEPTH, POOL_GROUPS, POOL_GC, POOL_GC), POOL_GC ** -0.5),
        "pool_scale": gain(ks[8], (DEPTH, D_POOL)),
        "sgu_ln_g": gain(ks[9], (DEPTH, D_SGU)),
        "sgu_ln_b": nrm(ks[10], (DEPTH, D_SGU), 0.02),
        "sgu_w": nrm(ks[11], (DEPTH, SGU_HEADS, SGU_CHUNK, SGU_CHUNK), 0.5 * SGU_CHUNK ** -0.5),
        "sgu_b": gain(ks[12], (DEPTH, SGU_HEADS, SGU_CHUNK)),
        "w_out": nrm(ks[13], (DEPTH, MIX, D_MODEL), MIX ** -0.5),
        "g_ffn": gain(ks[14], (DEPTH, D_MODEL)),
        "router_grp_w": nrm(ks[15], (DEPTH, D_MODEL, MOE_GROUPS), D_MODEL ** -0.5),
        "router_grp_b": nrm(ks[16], (DEPTH, MOE_GROUPS), 0.01),
        "router_exp_w": nrm(ks[17], (DEPTH, D_MODEL, N_EXPERTS), D_MODEL ** -0.5),
        "router_exp_b": nrm(ks[18], (DEPTH, N_EXPERTS), 0.01),
        "moe_w1": nrm(ks[19], (DEPTH, N_EXPERTS, D_MODEL, D_EXPERT), D_MODEL ** -0.5),
        "moe_w3": nrm(ks[20], (DEPTH, N_EXPERTS, D_MODEL, D_EXPERT), D_MODEL ** -0.5),
        "moe_w2": nrm(ks[21], (DEPTH, N_EXPERTS, D_EXPERT, D_MODEL), D_EXPERT ** -0.5),
        "g_ple": gain(ks[22], (DEPTH, D_MODEL)),
        "ple_gate_down": nrm(ks[23], (DEPTH, D_MODEL, PLE_DIM), D_MODEL ** -0.5),
        "ple_gate_up": nrm(ks[24], (DEPTH, PLE_DIM, D_MODEL), PLE_DIM ** -0.5),
        "ple_proj": nrm(ks[25], (DEPTH, PLE_DIM, D_MODEL), PLE_DIM ** -0.5),
        "g_final": gain(ks[26], (D_MODEL,)),
    }


def reference(x_prompt, x_sample, state_pool, p_prompt, p_sample, g_mix, w_in, w_pool, pool_scale,
              sgu_ln_g, sgu_ln_b, sgu_w, sgu_b, w_out, g_ffn, router_grp_w, router_grp_b,
              router_exp_w, router_exp_b, moe_w1, moe_w3, moe_w2, g_ple, ple_gate_down,
              ple_gate_up, ple_proj, g_final):
    xp, xs = x_prompt, x_sample
    zero_past = jnp.zeros((xp.shape[0], POOL_CACHE, D_POOL), xp.dtype)
    n_past_sample = min(POOL_CACHE, PAST_LEN)
    pool_p, pool_s, v_s = [], [], []
    for i in range(DEPTH):
        lw = (g_mix[i], w_in[i], w_pool[i], pool_scale[i], sgu_ln_g[i], sgu_ln_b[i], sgu_w[i], sgu_b[i],
              w_out[i], g_ffn[i], router_grp_w[i], router_grp_b[i], router_exp_w[i], router_exp_b[i],
              moe_w1[i], moe_w3[i], moe_w2[i], g_ple[i], ple_gate_down[i], ple_gate_up[i], ple_proj[i])
        xp, np_i, _ = block(xp, p_prompt[i], zero_past, 0, *lw)
        xs, ns_i, vs_i = block(xs, p_sample[i], state_pool[i], n_past_sample, *lw)
        pool_p.append(np_i)
        pool_s.append(ns_i)
        v_s.append(vs_i)
    y_prompt = rms_norm(xp, g_final)
    y_sample = rms_norm(xs, g_final)
    new_pool_prompt = jnp.stack(pool_p, axis=0)
    new_pool_sample = jnp.stack(pool_s, axis=0)
    new_sgu_v_sample = jnp.stack(v_s, axis=0)
    return (y_prompt, y_sample, new_pool_prompt, new_pool_sample, new_sgu_v_sample)
```

```python
import functools

import numpy as np
import jax
import jax.numpy as jnp
from jax import lax
from jax.experimental import pallas as pl
from jax.experimental.pallas import tpu as pltpu

PAST_LEN = 2048
POOL_WINDOWS = (2, 4, 8, 16)
POOL_CACHE = max(POOL_WINDOWS) - 1
POOL_PAD = 16
SGU_CHUNK = 128
SGU_HEADS = 8
MOE_GROUPS = 4
MOE_EXP_PER_GROUP = 4
N_EXPERTS = MOE_GROUPS * MOE_EXP_PER_GROUP
TOP_K = 2
EPS = 1e-6
ROUTER_ROWS = 32

F32 = jnp.float32
BF16 = jnp.bfloat16
MIB = 2 ** 20


def _params(semantics, vmem_mib):
    return pltpu.CompilerParams(dimension_semantics=semantics, vmem_limit_bytes=vmem_mib * MIB)


def _rms(x, g):
    return x * lax.rsqrt(jnp.mean(x * x, axis=-1, keepdims=True) + EPS) * g


def _gelu(x):
    return 0.5 * x * (1.0 + lax.erf(x * np.float32(np.sqrt(0.5))))


def _inproj_kernel(x_ref, g_ref, w_ref, z_ref, h_ref, *, n_raw):
    j = pl.program_id(1)

    @pl.when(j == 0)
    def _():
        h_ref[...] = _rms(x_ref[...], g_ref[...]).astype(BF16)

    acc = jnp.dot(h_ref[...], w_ref[...], preferred_element_type=F32)

    @pl.when(j < n_raw)
    def _():
        z_ref[...] = acc

    @pl.when(j >= n_raw)
    def _():
        z_ref[...] = _gelu(acc)


def _inproj(x, g, w, d_pool, tm, tn):
    t, d = x.shape
    n = w.shape[1]
    return pl.pallas_call(
        functools.partial(_inproj_kernel, n_raw=d_pool // tn),
        grid=(t // tm, n // tn),
        in_specs=[pl.BlockSpec((tm, d), lambda i, j: (i, 0)),
                  pl.BlockSpec((1, d), lambda i, j: (0, 0)),
                  pl.BlockSpec((d, tn), lambda i, j: (0, j))],
        out_specs=pl.BlockSpec((tm, tn), lambda i, j: (i, j)),
        out_shape=jax.ShapeDtypeStruct((t, n), F32),
        scratch_shapes=[pltpu.VMEM((tm, d), BF16)],
        compiler_params=_params(("parallel", "arbitrary"), 60),
        name="inproj",
    )(x, g, w)


def _mixer_kernel(*refs, ts, chunk, has_past, n_past, emit_v):
    it = iter(refs)
    za_ref, u_ref, v_ref = next(it), next(it), next(it)
    past_ref = next(it) if has_past else None
    wpool_ref, pscale_ref, lng_ref, lnb_ref, ws_ref, bst_ref = (next(it) for _ in range(6))
    m_ref, pool_ref = next(it), next(it)
    vout_ref = next(it) if emit_v else None
    ext_ref = next(it)

    s = pl.program_id(1)
    dp = za_ref.shape[1]
    gc = dp // len(POOL_WINDOWS)
    lo = POOL_PAD

    @pl.when(s == 0)
    def _():
        if has_past:
            ext_ref[0:lo - POOL_CACHE, :] = jnp.zeros((lo - POOL_CACHE, dp), F32)
            ext_ref[lo - POOL_CACHE:lo, :] = past_ref[...]
        else:
            ext_ref[0:lo, :] = jnp.zeros((lo, dp), F32)

    za = za_ref[...]
    ext_ref[lo:lo + ts, :] = za
    pos = s * ts + lax.broadcasted_iota(jnp.int32, (ts, 1), 0)
    for g, w in enumerate(POOL_WINDOWS):
        c0 = g * gc
        zg = za[:, c0:c0 + gc]
        win = zg
        for k in range(1, w):
            win = win + ext_ref[lo - k:lo - k + ts, c0:c0 + gc]
        cnt = jnp.minimum(w, pos + 1 + n_past).astype(F32)
        dlt = win * (1.0 / cnt) - zg
        y = jnp.dot(dlt.astype(BF16), wpool_ref[g], preferred_element_type=F32)
        m_ref[:, c0:c0 + gc] = (y * pscale_ref[:, c0:c0 + gc]).astype(BF16)

    tail = za[ts - POOL_CACHE:ts, :]
    pool_ref[...] = tail
    ext_ref[lo - POOL_CACHE:lo, :] = tail

    v = v_ref[...]
    mu = jnp.mean(v, axis=-1, keepdims=True)
    vc = v - mu
    var = jnp.mean(vc * vc, axis=-1, keepdims=True)
    vln = vc * lax.rsqrt(var + EPS) * lng_ref[...] + lnb_ref[...]
    if emit_v:
        vout_ref[...] = vln
    vb = vln.astype(BF16)
    u = u_ref[...]
    hd = u.shape[1] // SGU_HEADS
    tril = (lax.broadcasted_iota(jnp.int32, (chunk, chunk), 0)
            >= lax.broadcasted_iota(jnp.int32, (chunk, chunk), 1))
    for h in range(SGU_HEADS):
        wm = jnp.where(tril, ws_ref[h, 0:chunk, 0:chunk], 0.0).astype(BF16)
        bias = bst_ref[0:chunk, h:h + 1]
        for c in range(ts // chunk):
            rows = slice(c * chunk, (c + 1) * chunk)
            cols = slice(h * hd, (h + 1) * hd)
            mixed = jnp.dot(wm, vb[rows, cols], preferred_element_type=F32) + bias
            m_ref[rows, dp + h * hd:dp + (h + 1) * hd] = (u[rows, cols] * mixed).astype(BF16)


def _mixer(z, past, wpool, pscale, lng, lnb, ws, bst, *, layer, row0, n_seq, seq, ts, n_past, emit_v):
    t = z.shape[0]
    dp = pscale.shape[1]
    ds = lng.shape[1]
    assert dp == ds and z.shape[1] == dp + 2 * ds
    d = dp + ds
    n_tiles = seq // ts
    chunk = min(seq, SGU_CHUNK)
    assert ts % chunk == 0 and row0 % ts == 0
    rb0 = row0 // ts
    has_past = past is not None

    def zspec(col):
        return pl.BlockSpec((ts, dp), lambda b, s: (rb0 + b * n_tiles + s, col))

    def full(a):
        nd = a.ndim
        return pl.BlockSpec(a.shape, lambda b, s: (0,) * nd)

    in_specs = [zspec(0), zspec(1), zspec(2)]
    args = [z, z, z]
    if has_past:
        in_specs.append(pl.BlockSpec((None, None, POOL_CACHE, dp), lambda b, s: (layer, b, 0, 0)))
        args.append(past)
    small = [wpool, pscale, lng, lnb, ws, bst]
    in_specs += [full(a) for a in small]
    args += small

    out_shape = [jax.ShapeDtypeStruct((n_seq * seq, d), BF16),
                 jax.ShapeDtypeStruct((n_seq, POOL_CACHE, dp), F32)]
    out_specs = [pl.BlockSpec((ts, d), lambda b, s: (b * n_tiles + s, 0)),
                 pl.BlockSpec((None, POOL_CACHE, dp), lambda b, s: (b, 0, 0))]
    if emit_v:
        out_shape.append(jax.ShapeDtypeStruct((n_seq, seq, ds), F32))
        out_specs.append(pl.BlockSpec((None, ts, ds), lambda b, s: (b, s, 0)))
    del t
    return pl.pallas_call(
        functools.partial(_mixer_kernel, ts=ts, chunk=chunk, has_past=has_past,
                          n_past=n_past, emit_v=emit_v),
        grid=(n_seq, n_tiles),
        in_specs=in_specs,
        out_specs=out_specs,
        out_shape=out_shape,
        scratch_shapes=[pltpu.VMEM((POOL_PAD + ts, dp), F32)],
        compiler_params=_params(("parallel", "arbitrary"), 48),
        name="mixer_v" if emit_v else "mixer",
    )(*args)


def _outproj_kernel(m_ref, w_ref, x_ref, o_ref):
    o_ref[...] = x_ref[...] + jnp.dot(m_ref[...], w_ref[...], preferred_element_type=F32)


def _outproj(m, w, x, tm, tn):
    t, d = x.shape
    k = m.shape[1]
    return pl.pallas_call(
        _outproj_kernel,
        grid=(t // tm, d // tn),
        in_specs=[pl.BlockSpec((tm, k), lambda i, j: (i, 0)),
                  pl.BlockSpec((k, tn), lambda i, j: (0, j)),
                  pl.BlockSpec((tm, tn), lambda i, j: (i, j))],
        out_specs=pl.BlockSpec((tm, tn), lambda i, j: (i, j)),
        out_shape=jax.ShapeDtypeStruct((t, d), F32),
        input_output_aliases={2: 0},
        compiler_params=_params(("parallel", "arbitrary"), 48),
        name="outproj",
    )(m, w, x)


def _first_max(vals):
    mx = vals[0]
    for v in vals[1:]:
        mx = jnp.maximum(mx, v)
    idx = jnp.full(mx.shape, len(vals) - 1, jnp.int32)
    for j in range(len(vals) - 2, -1, -1):
        idx = jnp.where(vals[j] == mx, j, idx)
    return mx, idx


def _router_kernel(x_ref, g_ref, w_ref, b_ref, h_ref, ri_ref, rf_ref, cnt_ref, carry_ref):
    i = pl.program_id(0)
    tm = x_ref.shape[0]

    @pl.when(i == 0)
    def _():
        carry_ref[...] = jnp.zeros(carry_ref.shape, F32)

    h = _rms(x_ref[...], g_ref[...])
    h_ref[...] = h
    logits = lax.dot_general(w_ref[...], h, (((1,), (1,)), ((), ())),
                             precision=lax.Precision.HIGHEST,
                             preferred_element_type=F32) + b_ref[:, 0:1]
    row = [logits[j:j + 1, :] for j in range(MOE_GROUPS + N_EXPERTS)]
    lg = row[:MOE_GROUPS]
    gmax, grp = _first_max(lg)
    gsum = jnp.exp(lg[0] - gmax)
    for j in range(1, MOE_GROUPS):
        gsum = gsum + jnp.exp(lg[j] - gmax)
    pg = 1.0 / gsum
    ev = []
    for j in range(MOE_EXP_PER_GROUP):
        sel = row[MOE_GROUPS + (MOE_GROUPS - 1) * MOE_EXP_PER_GROUP + j]
        for gi in range(MOE_GROUPS - 2, -1, -1):
            sel = jnp.where(grp == gi, row[MOE_GROUPS + gi * MOE_EXP_PER_GROUP + j], sel)
        ev.append(sel)
    v1, i1 = _first_max(ev)
    rest = [jnp.where(i1 == j, -jnp.inf, ev[j]) for j in range(MOE_EXP_PER_GROUP)]
    v2, i2 = _first_max(rest)
    e21 = jnp.exp(v2 - v1)
    den = 1.0 + e21
    gate0 = pg * (1.0 / den)
    gate1 = pg * (e21 / den)
    eid0 = grp * MOE_EXP_PER_GROUP + i1
    eid1 = grp * MOE_EXP_PER_GROUP + i2

    erow = lax.broadcasted_iota(jnp.int32, (N_EXPERTS, tm), 0)
    c0 = erow == eid0
    c1 = erow == eid1
    hit = jnp.logical_or(c0, c1)
    upper = (lax.broadcasted_iota(jnp.int32, (tm, tm), 0)
             < lax.broadcasted_iota(jnp.int32, (tm, tm), 1))
    before = jnp.dot(hit.astype(BF16), upper.astype(BF16), preferred_element_type=F32)
    before = before + carry_ref[:, 0:1]
    r0 = jnp.sum(jnp.where(c0, before, 0.0), axis=0, keepdims=True)
    r1 = jnp.sum(jnp.where(c1, before, 0.0), axis=0, keepdims=True)
    carry_ref[...] = carry_ref[...] + jnp.sum(hit.astype(F32), axis=1, keepdims=True)
    cnt_ref[...] = carry_ref[...]

    ri_ref[...] = jnp.zeros(ri_ref.shape, jnp.int32)
    ri_ref[0:1, :] = eid0
    ri_ref[1:2, :] = eid1
    ri_ref[2:3, :] = r0.astype(jnp.int32)
    ri_ref[3:4, :] = r1.astype(jnp.int32)
    rf_ref[...] = jnp.zeros(rf_ref.shape, F32)
    rf_ref[0:1, :] = gate0
    rf_ref[1:2, :] = gate1


def _router(x, g, wr, br, tm):
    t, d = x.shape
    return pl.pallas_call(
        _router_kernel,
        grid=(t // tm,),
        in_specs=[pl.BlockSpec((tm, d), lambda i: (i, 0)),
                  pl.BlockSpec((1, d), lambda i: (0, 0)),
                  pl.BlockSpec((ROUTER_ROWS, d), lambda i: (0, 0)),
                  pl.BlockSpec((ROUTER_ROWS, 128), lambda i: (0, 0))],
        out_specs=[pl.BlockSpec((tm, d), lambda i: (i, 0)),
                   pl.BlockSpec((8, tm), lambda i: (0, i)),
                   pl.BlockSpec((8, tm), lambda i: (0, i)),
                   pl.BlockSpec((N_EXPERTS, 128), lambda i: (0, 0))],
        out_shape=[jax.ShapeDtypeStruct((t, d), F32),
                   jax.ShapeDtypeStruct((8, t), jnp.int32),
                   jax.ShapeDtypeStruct((8, t), F32),
                   jax.ShapeDtypeStruct((N_EXPERTS, 128), F32)],
        scratch_shapes=[pltpu.VMEM((N_EXPERTS, 128), F32)],
        compiler_params=_params(("arbitrary",), 56),
        name="router",
    )(x, g, wr, br)


def _row_copy(src_hbm, src_row, dst, dst_row, sem):
    return pltpu.make_async_copy(src_hbm.at[pl.ds(src_row, 1), :], dst.at[pl.ds(dst_row, 1), :], sem)


def _dispatch_kernel(dest_ref, h_hbm, xz_hbm, xbuf_hbm, sem, *, tq):
    del xz_hbm
    base = pl.program_id(0) * tq

    def issue(t, c):
        for k in range(TOP_K):
            _row_copy(h_hbm, base + t, xbuf_hbm, dest_ref[k, t], sem).start()
        return c

    lax.fori_loop(0, tq, issue, 0)

    def drain(t, c):
        for k in range(TOP_K):
            _row_copy(h_hbm, base + t, xbuf_hbm, dest_ref[k, t], sem).wait()
        return c

    lax.fori_loop(0, tq, drain, 0)


def _dispatch(dest, h, xzero, tq):
    t, d = h.shape
    return pl.pallas_call(
        functools.partial(_dispatch_kernel, tq=tq),
        grid=(t // tq,),
        in_specs=[pl.BlockSpec((TOP_K, tq), lambda i: (0, i), memory_space=pltpu.SMEM),
                  pl.BlockSpec(memory_space=pl.ANY),
                  pl.BlockSpec(memory_space=pl.ANY)],
        out_specs=pl.BlockSpec(memory_space=pl.ANY),
        out_shape=jax.ShapeDtypeStruct(xzero.shape, F32),
        scratch_shapes=[pltpu.SemaphoreType.DMA],
        input_output_aliases={2: 0},
        compiler_params=_params(("arbitrary",), 16),
        name="dispatch",
    )(dest, h, xzero)


def _expert_up_kernel(be_ref, nu_ref, x_ref, w1_ref, w3_ref, o_ref, xb_ref):
    del be_ref
    b = pl.program_id(0)
    f = pl.program_id(1)

    @pl.when(b < nu_ref[0])
    def _():
        @pl.when(f == 0)
        def _():
            xb_ref[...] = x_ref[...].astype(BF16)

        xb = xb_ref[...]
        h1 = jnp.dot(xb, w1_ref[...], preferred_element_type=F32)
        h3 = jnp.dot(xb, w3_ref[...], preferred_element_type=F32)
        o_ref[...] = (h1 * jax.nn.sigmoid(h1) * h3).astype(BF16)

    @pl.when(b >= nu_ref[0])
    def _():
        o_ref[...] = jnp.zeros(o_ref.shape, BF16)


def _expert_up(blk_exp, n_used, xbuf, w1, w3, bm, tf):
    r, d = xbuf.shape
    fdim = w1.shape[2]
    nf = fdim // tf

    def blk(b, nu):
        return jnp.minimum(b, nu[0] - 1)

    def fch(b, f, nu):
        return jnp.where(b < nu[0], f, nf - 1)

    return pl.pallas_call(
        _expert_up_kernel,
        grid_spec=pltpu.PrefetchScalarGridSpec(
            num_scalar_prefetch=2,
            grid=(r // bm, nf),
            in_specs=[pl.BlockSpec((bm, d), lambda b, f, be, nu: (blk(b, nu), 0)),
                      pl.BlockSpec((None, d, tf), lambda b, f, be, nu: (be[blk(b, nu)], 0, fch(b, f, nu))),
                      pl.BlockSpec((None, d, tf), lambda b, f, be, nu: (be[blk(b, nu)], 0, fch(b, f, nu)))],
            out_specs=pl.BlockSpec((bm, tf), lambda b, f, be, nu: (b, f)),
            scratch_shapes=[pltpu.VMEM((bm, d), BF16)]),
        out_shape=jax.ShapeDtypeStruct((r, fdim), BF16),
        compiler_params=_params(("arbitrary", "arbitrary"), 52),
        name="expert_up",
    )(blk_exp, n_used, xbuf, w1, w3)


def _expert_down_kernel(be_ref, nu_ref, h_ref, w2_ref, o_ref):
    del be_ref

    @pl.when(pl.program_id(0) < nu_ref[0])
    def _():
        o_ref[...] = jnp.dot(h_ref[...], w2_ref[...], preferred_element_type=F32)

    @pl.when(pl.program_id(0) >= nu_ref[0])
    def _():
        o_ref[...] = jnp.zeros(o_ref.shape, F32)


def _expert_down(blk_exp, n_used, hmid, w2, bm):
    r, fdim = hmid.shape
    d = w2.shape[2]

    def blk(b, nu):
        return jnp.minimum(b, nu[0] - 1)

    return pl.pallas_call(
        _expert_down_kernel,
        grid_spec=pltpu.PrefetchScalarGridSpec(
            num_scalar_prefetch=2,
            grid=(r // bm,),
            in_specs=[pl.BlockSpec((bm, fdim), lambda b, be, nu: (blk(b, nu), 0)),
                      pl.BlockSpec((None, fdim, d), lambda b, be, nu: (be[blk(b, nu)], 0, 0))],
            out_specs=pl.BlockSpec((bm, d), lambda b, be, nu: (b, 0))),
        out_shape=jax.ShapeDtypeStruct((r, d), F32),
        compiler_params=_params(("arbitrary",), 48),
        name="expert_down",
    )(blk_exp, n_used, hmid, w2)


def _combine_kernel(dest_ref, gate_ref, x_ref, p_ref, g_ref, wd_ref, wu_ref, wp_ref, y_hbm,
                    o_ref, ybuf_ref, sem):
    tq = x_ref.shape[0]

    def issue(t, c):
        for k in range(TOP_K):
            _row_copy(y_hbm, dest_ref[k, t], ybuf_ref.at[k], t, sem).start()
        return c

    lax.fori_loop(0, tq, issue, 0)

    def drain(t, c):
        for k in range(TOP_K):
            _row_copy(y_hbm, dest_ref[k, t], ybuf_ref.at[k], t, sem).wait()
        return c

    lax.fori_loop(0, tq, drain, 0)

    moe = ybuf_ref[0] * gate_ref[:, 0:1] + ybuf_ref[1] * gate_ref[:, 1:2]
    x = x_ref[...] + moe
    hn = _rms(x, g_ref[...]).astype(BF16)
    low = jnp.dot(hn, wd_ref[...], preferred_element_type=F32).astype(BF16)
    gate = jax.nn.sigmoid(jnp.dot(low, wu_ref[...], preferred_element_type=F32))
    emb = jnp.dot(p_ref[...].astype(BF16), wp_ref[...], preferred_element_type=F32)
    o_ref[...] = x + gate * emb


def _combine(dest, gates, x, p, g, wd, wu, wp, ybuf, tq):
    t, d = x.shape
    pd = p.shape[1]

    def full(a):
        return pl.BlockSpec(a.shape, lambda i: (0, 0))

    return pl.pallas_call(
        _combine_kernel,
        grid=(t // tq,),
        in_specs=[pl.BlockSpec((TOP_K, tq), lambda i: (0, i), memory_space=pltpu.SMEM),
                  pl.BlockSpec((tq, TOP_K), lambda i: (i, 0)),
                  pl.BlockSpec((tq, d), lambda i: (i, 0)),
                  pl.BlockSpec((tq, pd), lambda i: (i, 0)),
                  full(g), full(wd), full(wu), full(wp),
                  pl.BlockSpec(memory_space=pl.ANY)],
        out_specs=pl.BlockSpec((tq, d), lambda i: (i, 0)),
        out_shape=jax.ShapeDtypeStruct((t, d), F32),
        scratch_shapes=[pltpu.VMEM((TOP_K, tq, d), F32), pltpu.SemaphoreType.DMA],
        input_output_aliases={2: 0},
        compiler_params=_params(("arbitrary",), 48),
        name="combine",
    )(dest, gates, x, p, g, wd, wu, wp, ybuf)


def _final_kernel(x_ref, g_ref, o_ref):
    o_ref[...] = _rms(x_ref[...], g_ref[...])


def _final_norm(x, g, row0, rows, tm):
    d = x.shape[1]
    rb0 = row0 // tm
    return pl.pallas_call(
        _final_kernel,
        grid=(rows // tm,),
        in_specs=[pl.BlockSpec((tm, d), lambda i: (rb0 + i, 0)),
                  pl.BlockSpec((1, d), lambda i: (0, 0))],
        out_specs=pl.BlockSpec((tm, d), lambda i: (i, 0)),
        out_shape=jax.ShapeDtypeStruct((rows, d), F32),
        compiler_params=_params(("parallel",), 32),
        name="final_norm",
    )(x, g)


def _tile(n, pref):
    t = min(n, pref)
    while n % t:
        t //= 2
    return t


def kernel(x_prompt, x_sample, state_pool, p_prompt, p_sample, g_mix, w_in, w_pool, pool_scale, sgu_ln_g, sgu_ln_b, sgu_w, sgu_b, w_out, g_ffn, router_grp_w, router_grp_b, router_exp_w, router_exp_b, moe_w1, moe_w3, moe_w2, g_ple, ple_gate_down, ple_gate_up, ple_proj, g_final):
    depth = w_in.shape[0]
    nb_p, seq_p, d = x_prompt.shape
    nb_s, seq_s, _ = x_sample.shape
    dp = state_pool.shape[-1]
    t_p, t_s = nb_p * seq_p, nb_s * seq_s
    t = t_p + t_s
    n_past_s = min(POOL_CACHE, PAST_LEN)

    tm = _tile(t, 512)
    tn_in = _tile(dp, 1024)
    tn_out = _tile(d, 1024)
    ts_p = _tile(seq_p, 256)
    tq = _tile(t, 256)
    tq_d = _tile(t, 512)
    bm = _tile(t * TOP_K, 512)
    tf = _tile(moe_w1.shape[-1], 512)
    r_max = t * TOP_K + N_EXPERTS * bm
    nb_max = r_max // bm

    x = jnp.concatenate([x_prompt.reshape(t_p, d), x_sample.reshape(t_s, d)], axis=0)
    p_all = jnp.concatenate([p_prompt.reshape(depth, t_p, -1), p_sample.reshape(depth, t_s, -1)], axis=1)
    bf = lambda a: a.astype(BF16)
    n_log = MOE_GROUPS + N_EXPERTS
    wr = jnp.concatenate([router_grp_w, router_exp_w], axis=-1)
    wr = jnp.pad(jnp.swapaxes(wr, 1, 2), ((0, 0), (0, ROUTER_ROWS - n_log), (0, 0)))
    br = jnp.concatenate([router_grp_b, router_exp_b], axis=-1)
    br = jnp.broadcast_to(jnp.pad(br, ((0, 0), (0, ROUTER_ROWS - n_log)))[:, :, None],
                          (depth, ROUTER_ROWS, 128))
    bst = jnp.swapaxes(sgu_b, 1, 2)

    pool_p, pool_s, v_s = [], [], []
    for i in range(depth):
        row = lambda a: a[i][None, :]
        z = _inproj(x, row(g_mix), bf(w_in[i]), dp, tm, tn_in)
        mix_w = (bf(w_pool[i]), row(pool_scale), row(sgu_ln_g), row(sgu_ln_b), sgu_w[i], bst[i])
        m_p, np_i = _mixer(z, None, *mix_w, layer=i, row0=0, n_seq=nb_p, seq=seq_p, ts=ts_p,
                           n_past=0, emit_v=False)
        m_s, ns_i, vs_i = _mixer(z, state_pool, *mix_w, layer=i, row0=t_p, n_seq=nb_s, seq=seq_s,
                                 ts=seq_s, n_past=n_past_s, emit_v=True)
        pool_p.append(np_i)
        pool_s.append(ns_i)
        v_s.append(vs_i)
        m = jnp.concatenate([m_p, m_s], axis=0)
        x = _outproj(m, bf(w_out[i]), x, tm, tn_out)

        h, ri, rf, cnt = _router(x, row(g_ffn), wr[i], br[i], tm)
        counts = cnt[:, 0].astype(jnp.int32)
        padded = (counts + bm - 1) // bm * bm
        pend = jnp.cumsum(padded)
        pstart = pend - padded
        dest = pstart[ri[0:TOP_K]] + ri[TOP_K:2 * TOP_K]
        n_used = (pend[-1:] // bm).astype(jnp.int32)
        blk_exp = jnp.minimum(jnp.searchsorted(pend, jnp.arange(nb_max) * bm, side='right'),
                              N_EXPERTS - 1).astype(jnp.int32)
        gates = jnp.transpose(rf[0:TOP_K])

        xbuf = _dispatch(dest, h, jnp.zeros((r_max, d), F32), tq_d)
        hmid = _expert_up(blk_exp, n_used, xbuf, bf(moe_w1[i]), bf(moe_w3[i]), bm, tf)
        ybuf = _expert_down(blk_exp, n_used, hmid, bf(moe_w2[i]), bm)
        x = _combine(dest, gates, x, p_all[i], row(g_ple), bf(ple_gate_down[i]), bf(ple_gate_up[i]),
                     bf(ple_proj[i]), ybuf, tq)

    g_fin = g_final[None, :]
    y_prompt = _final_norm(x, g_fin, 0, t_p, tq).reshape(nb_p, seq_p, d)
    y_sample = _final_norm(x, g_fin, t_p, t_s, tq).reshape(nb_s, seq_s, d)
    return (y_prompt, y_sample, jnp.stack(pool_p), jnp.stack(pool_s), jnp.stack(v_s))
```

```python
import functools

import numpy as np
import jax
import jax.numpy as jnp
from jax import lax
from jax.experimental import pallas as pl
from jax.experimental.pallas import tpu as pltpu

PAST_LEN = 2048
POOL_WINDOWS = (2, 4, 8, 16)
POOL_CACHE = max(POOL_WINDOWS) - 1
POOL_PAD = 16
SGU_CHUNK = 128
SGU_HEADS = 8
MOE_GROUPS = 4
MOE_EXP_PER_GROUP = 4
N_EXPERTS = MOE_GROUPS * MOE_EXP_PER_GROUP
TOP_K = 2
EPS = 1e-6
ROUTER_ROWS = 32

F32 = jnp.float32
BF16 = jnp.bfloat16
MIB = 2 ** 20


def _params(semantics, vmem_mib):
    return pltpu.CompilerParams(dimension_semantics=semantics, vmem_limit_bytes=vmem_mib * MIB)


def _rms(x, g):
    return x * lax.rsqrt(jnp.mean(x * x, axis=-1, keepdims=True) + EPS) * g


def _gelu(x):
    return 0.5 * x * (1.0 + lax.erf(x * np.float32(np.sqrt(0.5))))


def _inproj_kernel(x_ref, g_ref, w_ref, z_ref, h_ref, *, n_raw):
    j = pl.program_id(1)

    @pl.when(j == 0)
    def _():
        h_ref[...] = _rms(x_ref[...], g_ref[...]).astype(BF16)

    acc = jnp.dot(h_ref[...], w_ref[...], preferred_element_type=F32)

    @pl.when(j < n_raw)
    def _():
        z_ref[...] = acc

    @pl.when(j >= n_raw)
    def _():
        z_ref[...] = _gelu(acc)


def _inproj(x, g, w, d_pool, tm, tn):
    t, d = x.shape
    n = w.shape[1]
    return pl.pallas_call(
        functools.partial(_inproj_kernel, n_raw=d_pool // tn),
        grid=(t // tm, n // tn),
        in_specs=[pl.BlockSpec((tm, d), lambda i, j: (i, 0)),
                  pl.BlockSpec((1, d), lambda i, j: (0, 0)),
                  pl.BlockSpec((d, tn), lambda i, j: (0, j))],
        out_specs=pl.BlockSpec((tm, tn), lambda i, j: (i, j)),
        out_shape=jax.ShapeDtypeStruct((t, n), F32),
        scratch_shapes=[pltpu.VMEM((tm, d), BF16)],
        compiler_params=_params(("parallel", "arbitrary"), 60),
        name="inproj",
    )(x, g, w)


def _mixer_kernel(*refs, ts, chunk, has_past, n_past, emit_v):
    it = iter(refs)
    za_ref, u_ref, v_ref = next(it), next(it), next(it)
    past_ref = next(it) if has_past else None
    wpool_ref, pscale_ref, lng_ref, lnb_ref, ws_ref, bst_ref = (next(it) for _ in range(6))
    m_ref, pool_ref = next(it), next(it)
    vout_ref = next(it) if emit_v else None
    ext_ref = next(it)

    s = pl.program_id(1)
    dp = za_ref.shape[1]
    gc = dp // len(POOL_WINDOWS)
    lo = POOL_PAD

    @pl.when(s == 0)
    def _():
        if has_past:
            ext_ref[0:lo - POOL_CACHE, :] = jnp.zeros((lo - POOL_CACHE, dp), F32)
            ext_ref[lo - POOL_CACHE:lo, :] = past_ref[...]
        else:
            ext_ref[0:lo, :] = jnp.zeros((lo, dp), F32)

    za = za_ref[...]
    ext_ref[lo:lo + ts, :] = za
    pos = s * ts + lax.broadcasted_iota(jnp.int32, (ts, 1), 0)
    for g, w in enumerate(POOL_WINDOWS):
        c0 = g * gc
        zg = za[:, c0:c0 + gc]
        win = zg
        for k in range(1, w):
            win = win + ext_ref[lo - k:lo - k + ts, c0:c0 + gc]
        cnt = jnp.minimum(w, pos + 1 + n_past).astype(F32)
        dlt = win * (1.0 / cnt) - zg
        y = jnp.dot(dlt.astype(BF16), wpool_ref[g], preferred_element_type=F32)
        m_ref[:, c0:c0 + gc] = (y * pscale_ref[:, c0:c0 + gc]).astype(BF16)

    tail = za[ts - POOL_CACHE:ts, :]
    pool_ref[...] = tail
    ext_ref[lo - POOL_CACHE:lo, :] = tail

    v = v_ref[...]
    mu = jnp.mean(v, axis=-1, keepdims=True)
    vc = v - mu
    var = jnp.mean(vc * vc, axis=-1, keepdims=True)
    vln = vc * lax.rsqrt(var + EPS) * lng_ref[...] + lnb_ref[...]
    if emit_v:
        vout_ref[...] = vln
    vb = vln.astype(BF16)
    u = u_ref[...]
    hd = u.shape[1] // SGU_HEADS
    tril = (lax.broadcasted_iota(jnp.int32, (chunk, chunk), 0)
            >= lax.broadcasted_iota(jnp.int32, (chunk, chunk), 1))
    for h in range(SGU_HEADS):
        wm = jnp.where(tril, ws_ref[h, 0:chunk, 0:chunk], 0.0).astype(BF16)
        bias = bst_ref[0:chunk, h:h + 1]
        for c in range(ts // chunk):
            rows = slice(c * chunk, (c + 1) * chunk)
            cols = slice(h * hd, (h + 1) * hd)
            mixed = jnp.dot(wm, vb[rows, cols], preferred_element_type=F32) + bias
            m_ref[rows, dp + h * hd:dp + (h + 1) * hd] = (u[rows, cols] * mixed).astype(BF16)


def _mixer(z, past, wpool, pscale, lng, lnb, ws, bst, *, layer, row0, n_seq, seq, ts, n_past, emit_v):
    t = z.shape[0]
    dp = pscale.shape[1]
    ds = lng.shape[1]
    assert dp == ds and z.shape[1] == dp + 2 * ds
    d = dp + ds
    n_tiles = seq // ts
    chunk = min(seq, SGU_CHUNK)
    assert ts % chunk == 0 and row0 % ts == 0
    rb0 = row0 // ts
    has_past = past is not None

    def zspec(col):
        return pl.BlockSpec((ts, dp), lambda b, s: (rb0 + b * n_tiles + s, col))

    def full(a):
        nd = a.ndim
        return pl.BlockSpec(a.shape, lambda b, s: (0,) * nd)

    in_specs = [zspec(0), zspec(1), zspec(2)]
    args = [z, z, z]
    if has_past:
        in_specs.append(pl.BlockSpec((None, None, POOL_CACHE, dp), lambda b, s: (layer, b, 0, 0)))
        args.append(past)
    small = [wpool, pscale, lng, lnb, ws, bst]
    in_specs += [full(a) for a in small]
    args += small

    out_shape = [jax.ShapeDtypeStruct((n_seq * seq, d), BF16),
                 jax.ShapeDtypeStruct((n_seq, POOL_CACHE, dp), F32)]
    out_specs = [pl.BlockSpec((ts, d), lambda b, s: (b * n_tiles + s, 0)),
                 pl.BlockSpec((None, POOL_CACHE, dp), lambda b, s: (b, 0, 0))]
    if emit_v:
        out_shape.append(jax.ShapeDtypeStruct((n_seq, seq, ds), F32))
        out_specs.append(pl.BlockSpec((None, ts, ds), lambda b, s: (b, s, 0)))
    del t
    return pl.pallas_call(
        functools.partial(_mixer_kernel, ts=ts, chunk=chunk, has_past=has_past,
                          n_past=n_past, emit_v=emit_v),
        grid=(n_seq, n_tiles),
        in_specs=in_specs,
        out_specs=out_specs,
        out_shape=out_shape,
        scratch_shapes=[pltpu.VMEM((POOL_PAD + ts, dp), F32)],
        compiler_params=_params(("parallel", "arbitrary"), 48),
        name="mixer_v" if emit_v else "mixer",
    )(*args)


def _outproj_kernel(m_ref, w_ref, x_ref, o_ref):
    o_ref[...] = x_ref[...] + jnp.dot(m_ref[...], w_ref[...], preferred_element_type=F32)


def _outproj(m, w, x, tm, tn):
    t, d = x.shape
    k = m.shape[1]
    return pl.pallas_call(
        _outproj_kernel,
        grid=(t // tm, d // tn),
        in_specs=[pl.BlockSpec((tm, k), lambda i, j: (i, 0)),
                  pl.BlockSpec((k, tn), lambda i, j: (0, j)),
                  pl.BlockSpec((tm, tn), lambda i, j: (i, j))],
        out_specs=pl.BlockSpec((tm, tn), lambda i, j: (i, j)),
        out_shape=jax.ShapeDtypeStruct((t, d), F32),
        input_output_aliases={2: 0},
        compiler_params=_params(("parallel", "arbitrary"), 48),
        name="outproj",
    )(m, w, x)


def _first_max(vals):
    mx = vals[0]
    for v in vals[1:]:
        mx = jnp.maximum(mx, v)
    idx = jnp.full(mx.shape, len(vals) - 1, jnp.int32)
    for j in range(len(vals) - 2, -1, -1):
        idx = jnp.where(vals[j] == mx, j, idx)
    return mx, idx


def _router_kernel(x_ref, g_ref, w_ref, b_ref, h_ref, ri_ref, rf_ref, cnt_ref, carry_ref):
    i = pl.program_id(0)
    tm = x_ref.shape[0]

    @pl.when(i == 0)
    def _():
        carry_ref[...] = jnp.zeros(carry_ref.shape, F32)

    h = _rms(x_ref[...], g_ref[...])
    half = h.shape[1] // 2
    hi = lax.bitcast_convert_type(h[:, :half].astype(BF16).astype(F32), jnp.uint32)
    lo = lax.bitcast_convert_type(h[:, half:].astype(BF16).astype(F32), jnp.uint32)
    h_ref[...] = hi | (lo >> 16)
    logits = lax.dot_general(w_ref[...], h, (((1,), (1,)), ((), ())),
                             precision=lax.Precision.HIGHEST,
                             preferred_element_type=F32) + b_ref[:, 0:1]
    row = [logits[j:j + 1, :] for j in range(MOE_GROUPS + N_EXPERTS)]
    lg = row[:MOE_GROUPS]
    gmax, grp = _first_max(lg)
    gsum = jnp.exp(lg[0] - gmax)
    for j in range(1, MOE_GROUPS):
        gsum = gsum + jnp.exp(lg[j] - gmax)
    pg = 1.0 / gsum
    ev = []
    for j in range(MOE_EXP_PER_GROUP):
        sel = row[MOE_GROUPS + (MOE_GROUPS - 1) * MOE_EXP_PER_GROUP + j]
        for gi in range(MOE_GROUPS - 2, -1, -1):
            sel = jnp.where(grp == gi, row[MOE_GROUPS + gi * MOE_EXP_PER_GROUP + j], sel)
        ev.append(sel)
    v1, i1 = _first_max(ev)
    rest = [jnp.where(i1 == j, -jnp.inf, ev[j]) for j in range(MOE_EXP_PER_GROUP)]
    v2, i2 = _first_max(rest)
    e21 = jnp.exp(v2 - v1)
    den = 1.0 + e21
    gate0 = pg * (1.0 / den)
    gate1 = pg * (e21 / den)
    eid0 = grp * MOE_EXP_PER_GROUP + i1
    eid1 = grp * MOE_EXP_PER_GROUP + i2

    erow = lax.broadcasted_iota(jnp.int32, (N_EXPERTS, tm), 0)
    c0 = erow == eid0
    c1 = erow == eid1
    hit = jnp.logical_or(c0, c1)
    upper = (lax.broadcasted_iota(jnp.int32, (tm, tm), 0)
             < lax.broadcasted_iota(jnp.int32, (tm, tm), 1))
    before = jnp.dot(hit.astype(BF16), upper.astype(BF16), preferred_element_type=F32)
    before = before + carry_ref[:, 0:1]
    r0 = jnp.sum(jnp.where(c0, before, 0.0), axis=0, keepdims=True)
    r1 = jnp.sum(jnp.where(c1, before, 0.0), axis=0, keepdims=True)
    carry_ref[...] = carry_ref[...] + jnp.sum(hit.astype(F32), axis=1, keepdims=True)
    cnt_ref[...] = carry_ref[...]

    ri_ref[...] = jnp.zeros(ri_ref.shape, jnp.int32)
    ri_ref[0:1, :] = eid0
    ri_ref[1:2, :] = eid1
    ri_ref[2:3, :] = r0.astype(jnp.int32)
    ri_ref[3:4, :] = r1.astype(jnp.int32)
    rf_ref[...] = jnp.zeros(rf_ref.shape, F32)
    rf_ref[0:1, :] = gate0
    rf_ref[1:2, :] = gate1


def _router(x, g, wr, br, tm):
    t, d = x.shape
    return pl.pallas_call(
        _router_kernel,
        grid=(t // tm,),
        in_specs=[pl.BlockSpec((tm, d), lambda i: (i, 0)),
                  pl.BlockSpec((1, d), lambda i: (0, 0)),
                  pl.BlockSpec((ROUTER_ROWS, d), lambda i: (0, 0)),
                  pl.BlockSpec((ROUTER_ROWS, 128), lambda i: (0, 0))],
        out_specs=[pl.BlockSpec((tm, d // 2), lambda i: (i, 0)),
                   pl.BlockSpec((8, tm), lambda i: (0, i)),
                   pl.BlockSpec((8, tm), lambda i: (0, i)),
                   pl.BlockSpec((N_EXPERTS, 128), lambda i: (0, 0))],
        out_shape=[jax.ShapeDtypeStruct((t, d // 2), jnp.uint32),
                   jax.ShapeDtypeStruct((8, t), jnp.int32),
                   jax.ShapeDtypeStruct((8, t), F32),
                   jax.ShapeDtypeStruct((N_EXPERTS, 128), F32)],
        scratch_shapes=[pltpu.VMEM((N_EXPERTS, 128), F32)],
        compiler_params=_params(("arbitrary",), 56),
        name="router",
    )(x, g, wr, br)


def _row_copy(src_hbm, src_row, dst, dst_row, sem):
    return pltpu.make_async_copy(src_hbm.at[pl.ds(src_row, 1), :], dst.at[pl.ds(dst_row, 1), :], sem)


def _route_index_kernel(eid0_ref, eid1_ref, rank0_ref, rank1_ref, pstart_ref,
                        src_ref, dest0_ref, dest1_ref, *, clear_steps, clear_rows, place_rows):
    i = pl.program_id(0)
    last = src_ref.shape[0] - 1

    @pl.when(i < clear_steps)
    def _():
        def clear(r, c):
            src_ref[jnp.minimum(i * clear_rows + r, last)] = 0
            return c

        lax.fori_loop(0, clear_rows, clear, 0)

    @pl.when(i >= clear_steps)
    def _():
        base = (i - clear_steps) * place_rows

        def place(j, c):
            t = base + j
            d0 = pstart_ref[eid0_ref[t]] + rank0_ref[t]
            d1 = pstart_ref[eid1_ref[t]] + rank1_ref[t]
            src_ref[d0] = t
            src_ref[d1] = t
            dest0_ref[t] = d0
            dest1_ref[t] = d1
            return c

        lax.fori_loop(0, place_rows, place, 0)


def _route_index(eid0, eid1, rank0, rank1, pstart, r_max, place_rows):
    t = eid0.shape[0]
    clear_rows = 4 * place_rows
    clear_steps = -(-r_max // clear_rows)
    smem = pl.BlockSpec(memory_space=pltpu.SMEM)
    return pl.pallas_call(
        functools.partial(_route_index_kernel, clear_steps=clear_steps, clear_rows=clear_rows,
                          place_rows=place_rows),
        grid=(clear_steps + t // place_rows,),
        in_specs=[smem] * 5,
        out_specs=[smem] * 3,
        out_shape=[jax.ShapeDtypeStruct((r_max,), jnp.int32),
                   jax.ShapeDtypeStruct((t,), jnp.int32),
                   jax.ShapeDtypeStruct((t,), jnp.int32)],
        compiler_params=pltpu.CompilerParams(dimension_semantics=("arbitrary",)),
        name="route_index",
    )(eid0, eid1, rank0, rank1, pstart)


def _load_expert_weights(pairs, stage_ref, sem, rows):
    jobs = [(w, res, c) for w, res in pairs for c in range(w.shape[0] // rows)]

    def copy(i):
        w, _, c = jobs[i]
        return pltpu.make_async_copy(w.at[pl.ds(c * rows, rows), :], stage_ref.at[i % 2], sem.at[i % 2])

    copy(0).start()
    for i, (_, res, c) in enumerate(jobs):
        if i + 1 < len(jobs):
            copy(i + 1).start()
        copy(i).wait()
        res[c * rows:(c + 1) * rows, :] = stage_ref[i % 2].astype(BF16)


def _expert_changed(be_ref, b):
    return jnp.logical_or(b == 0, be_ref[b] != be_ref[jnp.maximum(b - 1, 0)])


def _expert_up_kernel(be_ref, nu_ref, src_ref, hp_hbm, w1_hbm, w3_hbm, o_ref,
                      xg_ref, w1_ref, w3_ref, stage_ref, gsem, wsem, *, layer, stage_rows):
    b = pl.program_id(0)
    nu = nu_ref[0]
    bm = o_ref.shape[0]
    half = xg_ref.shape[2]

    def gather(blk, slot, wait):
        def body(r, c):
            cp = _row_copy(hp_hbm, src_ref[blk * bm + r], xg_ref.at[slot], r, gsem.at[slot])
            if wait:
                cp.wait()
            else:
                cp.start()
            return c

        lax.fori_loop(0, bm, body, 0, unroll=8)

    @pl.when(b == 0)
    def _():
        gather(0, 0, False)

    @pl.when(b < nu)
    def _():
        e = be_ref[b]

        @pl.when(_expert_changed(be_ref, b))
        def _():
            _load_expert_weights([(w1_hbm.at[layer, e], w1_ref), (w3_hbm.at[layer, e], w3_ref)],
                                 stage_ref, wsem, stage_rows)

        slot = b % 2

        @pl.when(b + 1 < nu)
        def _():
            gather(b + 1, 1 - slot, False)

        gather(b, slot, True)
        xs = xg_ref[slot]
        xa = lax.bitcast_convert_type(xs & jnp.uint32(0xFFFF0000), F32).astype(BF16)
        xb = lax.bitcast_convert_type(xs << 16, F32).astype(BF16)
        h1 = (jnp.dot(xa, w1_ref[0:half, :], preferred_element_type=F32)
              + jnp.dot(xb, w1_ref[half:2 * half, :], preferred_element_type=F32))
        h3 = (jnp.dot(xa, w3_ref[0:half, :], preferred_element_type=F32)
              + jnp.dot(xb, w3_ref[half:2 * half, :], preferred_element_type=F32))
        o_ref[...] = (h1 * jax.nn.sigmoid(h1) * h3).astype(BF16)

    @pl.when(b >= nu)
    def _():
        o_ref[...] = jnp.zeros(o_ref.shape, BF16)


def _expert_up(blk_exp, n_used, src, hp, w1, w3, layer, bm, stage_rows):
    r = src.shape[0]
    half = hp.shape[1]
    d, fdim = w1.shape[2], w1.shape[3]
    any_spec = pl.BlockSpec(memory_space=pl.ANY)
    return pl.pallas_call(
        functools.partial(_expert_up_kernel, layer=layer, stage_rows=stage_rows),
        grid_spec=pltpu.PrefetchScalarGridSpec(
            num_scalar_prefetch=3,
            grid=(r // bm,),
            in_specs=[any_spec, any_spec, any_spec],
            out_specs=pl.BlockSpec((bm, fdim), lambda b, be, nu, src: (b, 0)),
            scratch_shapes=[pltpu.VMEM((2, bm, half), jnp.uint32),
                            pltpu.VMEM((d, fdim), BF16),
                            pltpu.VMEM((d, fdim), BF16),
                            pltpu.VMEM((2, stage_rows, fdim), F32),
                            pltpu.SemaphoreType.DMA((2,)),
                            pltpu.SemaphoreType.DMA((2,))]),
        out_shape=jax.ShapeDtypeStruct((r, fdim), BF16),
        compiler_params=_params(("arbitrary",), 56),
        name="expert_up",
    )(blk_exp, n_used, src, hp, w1, w3)


def _expert_down_kernel(be_ref, nu_ref, h_ref, w2_hbm, o_ref, w2_ref, stage_ref, wsem,
                        *, layer, stage_rows):
    b = pl.program_id(0)

    @pl.when(b < nu_ref[0])
    def _():
        @pl.when(_expert_changed(be_ref, b))
        def _():
            _load_expert_weights([(w2_hbm.at[layer, be_ref[b]], w2_ref)], stage_ref, wsem, stage_rows)

        o_ref[...] = jnp.dot(h_ref[...], w2_ref[...], preferred_element_type=F32)

    @pl.when(b >= nu_ref[0])
    def _():
        o_ref[...] = jnp.zeros(o_ref.shape, F32)


def _expert_down(blk_exp, n_used, hmid, w2, layer, bm, stage_rows):
    r, fdim = hmid.shape
    d = w2.shape[3]
    return pl.pallas_call(
        functools.partial(_expert_down_kernel, layer=layer, stage_rows=stage_rows),
        grid_spec=pltpu.PrefetchScalarGridSpec(
            num_scalar_prefetch=2,
            grid=(r // bm,),
            in_specs=[pl.BlockSpec((bm, fdim), lambda b, be, nu: (jnp.minimum(b, nu[0] - 1), 0)),
                      pl.BlockSpec(memory_space=pl.ANY)],
            out_specs=pl.BlockSpec((bm, d), lambda b, be, nu: (b, 0)),
            scratch_shapes=[pltpu.VMEM((fdim, d), BF16),
                            pltpu.VMEM((2, stage_rows, d), F32),
                            pltpu.SemaphoreType.DMA((2,))]),
        out_shape=jax.ShapeDtypeStruct((r, d), F32),
        compiler_params=_params(("arbitrary",), 48),
        name="expert_down",
    )(blk_exp, n_used, hmid, w2)


def _combine_kernel(dest_ref, gate_ref, x_ref, p_ref, g_ref, wd_ref, wu_ref, wp_ref, y_hbm,
                    o_ref, ybuf_ref, sem):
    tq = x_ref.shape[0]

    def issue(t, c):
        for k in range(TOP_K):
            _row_copy(y_hbm, dest_ref[k, t], ybuf_ref.at[k], t, sem).start()
        return c

    lax.fori_loop(0, tq, issue, 0, unroll=4)

    def drain(t, c):
        for k in range(TOP_K):
            _row_copy(y_hbm, dest_ref[k, t], ybuf_ref.at[k], t, sem).wait()
        return c

    lax.fori_loop(0, tq, drain, 0, unroll=4)

    moe = ybuf_ref[0] * gate_ref[:, 0:1] + ybuf_ref[1] * gate_ref[:, 1:2]
    x = x_ref[...] + moe
    hn = _rms(x, g_ref[...]).astype(BF16)
    low = jnp.dot(hn, wd_ref[...], preferred_element_type=F32).astype(BF16)
    gate = jax.nn.sigmoid(jnp.dot(low, wu_ref[...], preferred_element_type=F32))
    emb = jnp.dot(p_ref[...].astype(BF16), wp_ref[...], preferred_element_type=F32)
    o_ref[...] = x + gate * emb


def _combine(dest, gates, x, p, g, wd, wu, wp, ybuf, tq):
    t, d = x.shape
    pd = p.shape[1]

    def full(a):
        return pl.BlockSpec(a.shape, lambda i: (0, 0))

    return pl.pallas_call(
        _combine_kernel,
        grid=(t // tq,),
        in_specs=[pl.BlockSpec((TOP_K, tq), lambda i: (0, i), memory_space=pltpu.SMEM),
                  pl.BlockSpec((tq, TOP_K), lambda i: (i, 0)),
                  pl.BlockSpec((tq, d), lambda i: (i, 0)),
                  pl.BlockSpec((tq, pd), lambda i: (i, 0)),
                  full(g), full(wd), full(wu), full(wp),
                  pl.BlockSpec(memory_space=pl.ANY)],
        out_specs=pl.BlockSpec((tq, d), lambda i: (i, 0)),
        out_shape=jax.ShapeDtypeStruct((t, d), F32),
        scratch_shapes=[pltpu.VMEM((TOP_K, tq, d), F32), pltpu.SemaphoreType.DMA],
        input_output_aliases={2: 0},
        compiler_params=_params(("arbitrary",), 48),
        name="combine",
    )(dest, gates, x, p, g, wd, wu, wp, ybuf)


def _final_kernel(x_ref, g_ref, o_ref):
    o_ref[...] = _rms(x_ref[...], g_ref[...])


def _final_norm(x, g, row0, rows, tm):
    d = x.shape[1]
    rb0 = row0 // tm
    return pl.pallas_call(
        _final_kernel,
        grid=(rows // tm,),
        in_specs=[pl.BlockSpec((tm, d), lambda i: (rb0 + i, 0)),
                  pl.BlockSpec((1, d), lambda i: (0, 0))],
        out_specs=pl.BlockSpec((tm, d), lambda i: (i, 0)),
        out_shape=jax.ShapeDtypeStruct((rows, d), F32),
        compiler_params=_params(("parallel",), 32),
        name="final_norm",
    )(x, g)


def _tile(n, pref):
    t = min(n, pref)
    while n % t:
        t //= 2
    return t


def kernel(x_prompt, x_sample, state_pool, p_prompt, p_sample, g_mix, w_in, w_pool, pool_scale, sgu_ln_g, sgu_ln_b, sgu_w, sgu_b, w_out, g_ffn, router_grp_w, router_grp_b, router_exp_w, router_exp_b, moe_w1, moe_w3, moe_w2, g_ple, ple_gate_down, ple_gate_up, ple_proj, g_final):
    depth = w_in.shape[0]
    nb_p, seq_p, d = x_prompt.shape
    nb_s, seq_s, _ = x_sample.shape
    dp = state_pool.shape[-1]
    t_p, t_s = nb_p * seq_p, nb_s * seq_s
    t = t_p + t_s
    n_past_s = min(POOL_CACHE, PAST_LEN)

    tm = _tile(t, 512)
    tn_in = _tile(dp, 1024)
    tn_out = _tile(d, 1024)
    ts_p = _tile(seq_p, 256)
    tq = _tile(t, 256)
    bm = _tile(t * TOP_K, 512)
    stage_up = _tile(d, 512)
    stage_down = _tile(moe_w2.shape[2], 128)
    r_max = t * TOP_K + N_EXPERTS * bm
    nb_max = r_max // bm

    x = jnp.concatenate([x_prompt.reshape(t_p, d), x_sample.reshape(t_s, d)], axis=0)
    p_all = jnp.concatenate([p_prompt.reshape(depth, t_p, -1), p_sample.reshape(depth, t_s, -1)], axis=1)
    bf = lambda a: a.astype(BF16)
    n_log = MOE_GROUPS + N_EXPERTS
    wr = jnp.concatenate([router_grp_w, router_exp_w], axis=-1)
    wr = jnp.pad(jnp.swapaxes(wr, 1, 2), ((0, 0), (0, ROUTER_ROWS - n_log), (0, 0)))
    br = jnp.concatenate([router_grp_b, router_exp_b], axis=-1)
    br = jnp.broadcast_to(jnp.pad(br, ((0, 0), (0, ROUTER_ROWS - n_log)))[:, :, None],
                          (depth, ROUTER_ROWS, 128))
    bst = jnp.swapaxes(sgu_b, 1, 2)

    pool_p, pool_s, v_s = [], [], []
    for i in range(depth):
        row = lambda a: a[i][None, :]
        z = _inproj(x, row(g_mix), bf(w_in[i]), dp, tm, tn_in)
        mix_w = (bf(w_pool[i]), row(pool_scale), row(sgu_ln_g), row(sgu_ln_b), sgu_w[i], bst[i])
        m_p, np_i = _mixer(z, None, *mix_w, layer=i, row0=0, n_seq=nb_p, seq=seq_p, ts=ts_p,
                           n_past=0, emit_v=False)
        m_s, ns_i, vs_i = _mixer(z, state_pool, *mix_w, layer=i, row0=t_p, n_seq=nb_s, seq=seq_s,
                                 ts=seq_s, n_past=n_past_s, emit_v=True)
        pool_p.append(np_i)
        pool_s.append(ns_i)
        v_s.append(vs_i)
        m = jnp.concatenate([m_p, m_s], axis=0)
        x = _outproj(m, bf(w_out[i]), x, tm, tn_out)

        hp, ri, rf, cnt = _router(x, row(g_ffn), wr[i], br[i], tm)
        counts = cnt[:, 0].astype(jnp.int32)
        padded = (counts + bm - 1) // bm * bm
        pend = jnp.cumsum(padded)
        pstart = pend - padded
        n_used = (pend[-1:] // bm).astype(jnp.int32)
        blk_first = jnp.arange(nb_max, dtype=jnp.int32) * bm
        blk_exp = jnp.minimum(jnp.sum((pend[None, :] <= blk_first[:, None]).astype(jnp.int32), axis=1),
                              N_EXPERTS - 1)
        src, dest0, dest1 = _route_index(ri[0], ri[1], ri[2], ri[3], pstart, r_max, tm)
        dest = jnp.stack([dest0, dest1])
        gates = jnp.transpose(rf[0:TOP_K])

        hmid = _expert_up(blk_exp, n_used, src, hp, moe_w1, moe_w3, i, bm, stage_up)
        ybuf = _expert_down(blk_exp, n_used, hmid, moe_w2, i, bm, stage_down)
        x = _combine(dest, gates, x, p_all[i], row(g_ple), bf(ple_gate_down[i]), bf(ple_gate_up[i]),
                     bf(ple_proj[i]), ybuf, tq)

    g_fin = g_final[None, :]
    y_prompt = _final_norm(x, g_fin, 0, t_p, tq).reshape(nb_p, seq_p, d)
    y_sample = _final_norm(x, g_fin, t_p, t_s, tq).reshape(nb_s, seq_s, d)
    return (y_prompt, y_sample, jnp.stack(pool_p), jnp.stack(pool_s), jnp.stack(v_s))
```

```python
import functools

import numpy as np
import jax
import jax.numpy as jnp
from jax import lax
from jax.experimental import pallas as pl
from jax.experimental.pallas import tpu as pltpu

PAST_LEN = 2048
POOL_WINDOWS = (2, 4, 8, 16)
POOL_CACHE = max(POOL_WINDOWS) - 1
POOL_PAD = 16
SGU_CHUNK = 128
SGU_HEADS = 8
MOE_GROUPS = 4
MOE_EXP_PER_GROUP = 4
N_EXPERTS = MOE_GROUPS * MOE_EXP_PER_GROUP
TOP_K = 2
EPS = 1e-6
ROUTER_ROWS = 32

F32 = jnp.float32
BF16 = jnp.bfloat16
MIB = 2 ** 20


def _params(semantics, vmem_mib):
    return pltpu.CompilerParams(dimension_semantics=semantics, vmem_limit_bytes=vmem_mib * MIB)


def _rms(x, g):
    return x * lax.rsqrt(jnp.mean(x * x, axis=-1, keepdims=True) + EPS) * g


def _gelu(x):
    return 0.5 * x * (1.0 + lax.erf(x * np.float32(np.sqrt(0.5))))


def _inproj_kernel(x_ref, g_ref, w_ref, z_ref, h_ref, *, n_raw):
    j = pl.program_id(1)

    @pl.when(j == 0)
    def _():
        h_ref[...] = _rms(x_ref[...], g_ref[...]).astype(BF16)

    acc = jnp.dot(h_ref[...], w_ref[...], preferred_element_type=F32)

    @pl.when(j < n_raw)
    def _():
        z_ref[...] = acc

    @pl.when(j >= n_raw)
    def _():
        z_ref[...] = _gelu(acc)


def _inproj(x, g, w, d_pool, tm, tn):
    t, d = x.shape
    n = w.shape[1]
    return pl.pallas_call(
        functools.partial(_inproj_kernel, n_raw=d_pool // tn),
        grid=(t // tm, n // tn),
        in_specs=[pl.BlockSpec((tm, d), lambda i, j: (i, 0)),
                  pl.BlockSpec((1, d), lambda i, j: (0, 0)),
                  pl.BlockSpec((d, tn), lambda i, j: (0, j))],
        out_specs=pl.BlockSpec((tm, tn), lambda i, j: (i, j)),
        out_shape=jax.ShapeDtypeStruct((t, n), F32),
        scratch_shapes=[pltpu.VMEM((tm, d), BF16)],
        compiler_params=_params(("parallel", "arbitrary"), 60),
        name="inproj",
    )(x, g, w)


def _mixer_kernel(*refs, ts, chunk, has_past, n_past, emit_v):
    it = iter(refs)
    za_ref, u_ref, v_ref = next(it), next(it), next(it)
    past_ref = next(it) if has_past else None
    wpool_ref, pscale_ref, lng_ref, lnb_ref, ws_ref, bst_ref = (next(it) for _ in range(6))
    m_ref, pool_ref = next(it), next(it)
    vout_ref = next(it) if emit_v else None
    ext_ref = next(it)

    s = pl.program_id(1)
    dp = za_ref.shape[1]
    gc = dp // len(POOL_WINDOWS)
    lo = POOL_PAD

    @pl.when(s == 0)
    def _():
        if has_past:
            ext_ref[0:lo - POOL_CACHE, :] = jnp.zeros((lo - POOL_CACHE, dp), F32)
            ext_ref[lo - POOL_CACHE:lo, :] = past_ref[...]
        else:
            ext_ref[0:lo, :] = jnp.zeros((lo, dp), F32)

    za = za_ref[...]
    ext_ref[lo:lo + ts, :] = za
    pos = s * ts + lax.broadcasted_iota(jnp.int32, (ts, 1), 0)
    for g, w in enumerate(POOL_WINDOWS):
        c0 = g * gc
        zg = za[:, c0:c0 + gc]
        win = zg
        for k in range(1, w):
            win = win + ext_ref[lo - k:lo - k + ts, c0:c0 + gc]
        cnt = jnp.minimum(w, pos + 1 + n_past).astype(F32)
        dlt = win * (1.0 / cnt) - zg
        y = jnp.dot(dlt.astype(BF16), wpool_ref[g], preferred_element_type=F32)
        m_ref[:, c0:c0 + gc] = (y * pscale_ref[:, c0:c0 + gc]).astype(BF16)

    tail = za[ts - POOL_CACHE:ts, :]
    pool_ref[...] = tail
    ext_ref[lo - POOL_CACHE:lo, :] = tail

    v = v_ref[...]
    mu = jnp.mean(v, axis=-1, keepdims=True)
    vc = v - mu
    var = jnp.mean(vc * vc, axis=-1, keepdims=True)
    vln = vc * lax.rsqrt(var + EPS) * lng_ref[...] + lnb_ref[...]
    if emit_v:
        vout_ref[...] = vln
    vb = vln.astype(BF16)
    u = u_ref[...]
    hd = u.shape[1] // SGU_HEADS
    tril = (lax.broadcasted_iota(jnp.int32, (chunk, chunk), 0)
            >= lax.broadcasted_iota(jnp.int32, (chunk, chunk), 1))
    for h in range(SGU_HEADS):
        wm = jnp.where(tril, ws_ref[h, 0:chunk, 0:chunk], 0.0).astype(BF16)
        bias = bst_ref[0:chunk, h:h + 1]
        for c in range(ts // chunk):
            rows = slice(c * chunk, (c + 1) * chunk)
            cols = slice(h * hd, (h + 1) * hd)
            mixed = jnp.dot(wm, vb[rows, cols], preferred_element_type=F32) + bias
            m_ref[rows, dp + h * hd:dp + (h + 1) * hd] = (u[rows, cols] * mixed).astype(BF16)


def _mixer(z, past, wpool, pscale, lng, lnb, ws, bst, *, layer, row0, n_seq, seq, ts, n_past, emit_v):
    t = z.shape[0]
    dp = pscale.shape[1]
    ds = lng.shape[1]
    assert dp == ds and z.shape[1] == dp + 2 * ds
    d = dp + ds
    n_tiles = seq // ts
    chunk = min(seq, SGU_CHUNK)
    assert ts % chunk == 0 and row0 % ts == 0
    rb0 = row0 // ts
    has_past = past is not None

    def zspec(col):
        return pl.BlockSpec((ts, dp), lambda b, s: (rb0 + b * n_tiles + s, col))

    def full(a):
        nd = a.ndim
        return pl.BlockSpec(a.shape, lambda b, s: (0,) * nd)

    in_specs = [zspec(0), zspec(1), zspec(2)]
    args = [z, z, z]
    if has_past:
        in_specs.append(pl.BlockSpec((None, None, POOL_CACHE, dp), lambda b, s: (layer, b, 0, 0)))
        args.append(past)
    small = [wpool, pscale, lng, lnb, ws, bst]
    in_specs += [full(a) for a in small]
    args += small

    out_shape = [jax.ShapeDtypeStruct((n_seq * seq, d), BF16),
                 jax.ShapeDtypeStruct((n_seq, POOL_CACHE, dp), F32)]
    out_specs = [pl.BlockSpec((ts, d), lambda b, s: (b * n_tiles + s, 0)),
                 pl.BlockSpec((None, POOL_CACHE, dp), lambda b, s: (b, 0, 0))]
    if emit_v:
        out_shape.append(jax.ShapeDtypeStruct((n_seq, seq, ds), F32))
        out_specs.append(pl.BlockSpec((None, ts, ds), lambda b, s: (b, s, 0)))
    del t
    return pl.pallas_call(
        functools.partial(_mixer_kernel, ts=ts, chunk=chunk, has_past=has_past,
                          n_past=n_past, emit_v=emit_v),
        grid=(n_seq, n_tiles),
        in_specs=in_specs,
        out_specs=out_specs,
        out_shape=out_shape,
        scratch_shapes=[pltpu.VMEM((POOL_PAD + ts, dp), F32)],
        compiler_params=_params(("parallel", "arbitrary"), 48),
        name="mixer_v" if emit_v else "mixer",
    )(*args)


def _outproj_kernel(ma_ref, mb_ref, w_ref, x_ref, o_ref, *, tiles_a):
    i = pl.program_id(0)

    @pl.when(i < tiles_a)
    def _():
        o_ref[...] = x_ref[...] + jnp.dot(ma_ref[...], w_ref[...], preferred_element_type=F32)

    @pl.when(i >= tiles_a)
    def _():
        o_ref[...] = x_ref[...] + jnp.dot(mb_ref[...], w_ref[...], preferred_element_type=F32)


def _outproj(m_a, m_b, w, x, tm, tn):
    t, d = x.shape
    k = m_a.shape[1]
    tiles_a, tiles_b = m_a.shape[0] // tm, m_b.shape[0] // tm
    assert tiles_a * tm == m_a.shape[0] and tiles_b * tm == m_b.shape[0] and (tiles_a + tiles_b) * tm == t
    return pl.pallas_call(
        functools.partial(_outproj_kernel, tiles_a=tiles_a),
        grid=(t // tm, d // tn),
        in_specs=[pl.BlockSpec((tm, k), lambda i, j: (jnp.minimum(i, tiles_a - 1), 0)),
                  pl.BlockSpec((tm, k), lambda i, j: (jnp.maximum(i - tiles_a, 0), 0)),
                  pl.BlockSpec((k, tn), lambda i, j: (0, j)),
                  pl.BlockSpec((tm, tn), lambda i, j: (i, j))],
        out_specs=pl.BlockSpec((tm, tn), lambda i, j: (i, j)),
        out_shape=jax.ShapeDtypeStruct((t, d), F32),
        input_output_aliases={3: 0},
        compiler_params=_params(("parallel", "arbitrary"), 52),
        name="outproj",
    )(m_a, m_b, w, x)


def _first_max(vals):
    mx = vals[0]
    for v in vals[1:]:
        mx = jnp.maximum(mx, v)
    idx = jnp.full(mx.shape, len(vals) - 1, jnp.int32)
    for j in range(len(vals) - 2, -1, -1):
        idx = jnp.where(vals[j] == mx, j, idx)
    return mx, idx


def _router_kernel(x_ref, g_ref, w_ref, b_ref, h_ref, ri_ref, rf_ref, cnt_ref, carry_ref):
    i = pl.program_id(0)
    tm = x_ref.shape[0]

    @pl.when(i == 0)
    def _():
        carry_ref[...] = jnp.zeros(carry_ref.shape, F32)

    h = _rms(x_ref[...], g_ref[...])
    half = h.shape[1] // 2
    hi = lax.bitcast_convert_type(h[:, :half].astype(BF16).astype(F32), jnp.uint32)
    lo = lax.bitcast_convert_type(h[:, half:].astype(BF16).astype(F32), jnp.uint32)
    h_ref[...] = hi | (lo >> 16)
    logits = lax.dot_general(w_ref[...], h, (((1,), (1,)), ((), ())),
                             precision=lax.Precision.HIGHEST,
                             preferred_element_type=F32) + b_ref[:, 0:1]
    row = [logits[j:j + 1, :] for j in range(MOE_GROUPS + N_EXPERTS)]
    lg = row[:MOE_GROUPS]
    gmax, grp = _first_max(lg)
    gsum = jnp.exp(lg[0] - gmax)
    for j in range(1, MOE_GROUPS):
        gsum = gsum + jnp.exp(lg[j] - gmax)
    pg = 1.0 / gsum
    ev = []
    for j in range(MOE_EXP_PER_GROUP):
        sel = row[MOE_GROUPS + (MOE_GROUPS - 1) * MOE_EXP_PER_GROUP + j]
        for gi in range(MOE_GROUPS - 2, -1, -1):
            sel = jnp.where(grp == gi, row[MOE_GROUPS + gi * MOE_EXP_PER_GROUP + j], sel)
        ev.append(sel)
    v1, i1 = _first_max(ev)
    rest = [jnp.where(i1 == j, -jnp.inf, ev[j]) for j in range(MOE_EXP_PER_GROUP)]
    v2, i2 = _first_max(rest)
    e21 = jnp.exp(v2 - v1)
    den = 1.0 + e21
    gate0 = pg * (1.0 / den)
    gate1 = pg * (e21 / den)
    eid0 = grp * MOE_EXP_PER_GROUP + i1
    eid1 = grp * MOE_EXP_PER_GROUP + i2

    erow = lax.broadcasted_iota(jnp.int32, (N_EXPERTS, tm), 0)
    c0 = erow == eid0
    c1 = erow == eid1
    hit = jnp.logical_or(c0, c1)
    upper = (lax.broadcasted_iota(jnp.int32, (tm, tm), 0)
             < lax.broadcasted_iota(jnp.int32, (tm, tm), 1))
    before = jnp.dot(hit.astype(BF16), upper.astype(BF16), preferred_element_type=F32)
    before = before + carry_ref[:, 0:1]
    r0 = jnp.sum(jnp.where(c0, before, 0.0), axis=0, keepdims=True)
    r1 = jnp.sum(jnp.where(c1, before, 0.0), axis=0, keepdims=True)
    carry_ref[...] = carry_ref[...] + jnp.sum(hit.astype(F32), axis=1, keepdims=True)
    cnt_ref[...] = carry_ref[...]

    ri_ref[...] = jnp.zeros(ri_ref.shape, jnp.int32)
    ri_ref[0:1, :] = eid0
    ri_ref[1:2, :] = eid1
    ri_ref[2:3, :] = r0.astype(jnp.int32)
    ri_ref[3:4, :] = r1.astype(jnp.int32)
    rf_ref[...] = jnp.zeros(rf_ref.shape, F32)
    rf_ref[0:1, :] = gate0
    rf_ref[1:2, :] = gate1


def _router(x, g, wr, br, tm):
    t, d = x.shape
    return pl.pallas_call(
        _router_kernel,
        grid=(t // tm,),
        in_specs=[pl.BlockSpec((tm, d), lambda i: (i, 0)),
                  pl.BlockSpec((1, d), lambda i: (0, 0)),
                  pl.BlockSpec((ROUTER_ROWS, d), lambda i: (0, 0)),
                  pl.BlockSpec((ROUTER_ROWS, 128), lambda i: (0, 0))],
        out_specs=[pl.BlockSpec((tm, d // 2), lambda i: (i, 0)),
                   pl.BlockSpec((8, tm), lambda i: (0, i)),
                   pl.BlockSpec((8, tm), lambda i: (0, i)),
                   pl.BlockSpec((N_EXPERTS, 128), lambda i: (0, 0))],
        out_shape=[jax.ShapeDtypeStruct((t, d // 2), jnp.uint32),
                   jax.ShapeDtypeStruct((8, t), jnp.int32),
                   jax.ShapeDtypeStruct((8, t), F32),
                   jax.ShapeDtypeStruct((N_EXPERTS, 128), F32)],
        scratch_shapes=[pltpu.VMEM((N_EXPERTS, 128), F32)],
        compiler_params=_params(("arbitrary",), 56),
        name="router",
    )(x, g, wr, br)


def _row_copy(src_hbm, src_row, dst, dst_row, sem):
    return pltpu.make_async_copy(src_hbm.at[pl.ds(src_row, 1), :], dst.at[pl.ds(dst_row, 1), :], sem)


def _route_index_kernel(dest_ref, src_ref, *, clear_steps, clear_rows, place_rows):
    i = pl.program_id(0)
    last = src_ref.shape[0] - 1
    t_total = dest_ref.shape[0] // TOP_K

    @pl.when(i < clear_steps)
    def _():
        def clear(r, c):
            src_ref[jnp.minimum(i * clear_rows + r, last)] = 0
            return c

        lax.fori_loop(0, clear_rows, clear, 0, unroll=8)

    @pl.when(i >= clear_steps)
    def _():
        base = (i - clear_steps) * place_rows

        def place(j, c):
            t = base + j
            for k in range(TOP_K):
                src_ref[dest_ref[k * t_total + t]] = t
            return c

        lax.fori_loop(0, place_rows, place, 0, unroll=8)


def _route_index(dest, r_max, place_rows):
    t = dest.shape[0] // TOP_K
    clear_rows = 4 * place_rows
    clear_steps = -(-r_max // clear_rows)
    smem = pl.BlockSpec(memory_space=pltpu.SMEM)
    return pl.pallas_call(
        functools.partial(_route_index_kernel, clear_steps=clear_steps, clear_rows=clear_rows,
                          place_rows=place_rows),
        grid=(clear_steps + t // place_rows,),
        in_specs=[smem],
        out_specs=smem,
        out_shape=jax.ShapeDtypeStruct((r_max,), jnp.int32),
        compiler_params=pltpu.CompilerParams(dimension_semantics=("arbitrary",)),
        name="route_index",
    )(dest)


def _load_expert_weights(pairs, stage_ref, sem, rows):
    jobs = [(w, res, c) for w, res in pairs for c in range(w.shape[0] // rows)]

    def copy(i):
        w, _, c = jobs[i]
        return pltpu.make_async_copy(w.at[pl.ds(c * rows, rows), :], stage_ref.at[i % 2], sem.at[i % 2])

    copy(0).start()
    for i, (_, res, c) in enumerate(jobs):
        if i + 1 < len(jobs):
            copy(i + 1).start()
        copy(i).wait()
        res[c * rows:(c + 1) * rows, :] = stage_ref[i % 2].astype(BF16)


def _expert_changed(be_ref, b):
    return jnp.logical_or(b == 0, be_ref[b] != be_ref[jnp.maximum(b - 1, 0)])


def _expert_up_kernel(be_ref, nu_ref, src_ref, hp_hbm, w1_hbm, w3_hbm, o_ref,
                      xg_ref, xa_ref, xb_ref, w1_ref, w3_ref, stage_ref, gsem, wsem,
                      *, layer, stage_rows):
    b = pl.program_id(0)
    nu = nu_ref[0]
    bm = o_ref.shape[0]
    half = xg_ref.shape[2]

    def gather(blk, slot, wait, unroll):
        def body(r, c):
            cp = _row_copy(hp_hbm, src_ref[blk * bm + r], xg_ref.at[slot], r, gsem.at[slot])
            if wait:
                cp.wait()
            else:
                cp.start()
            return c

        lax.fori_loop(0, bm, body, 0, unroll=unroll)

    @pl.when(b < nu)
    def _():
        e = be_ref[b]
        slot = b % 2

        @pl.when(b == 0)
        def _():
            gather(0, 0, False, 8)

        @pl.when(_expert_changed(be_ref, b))
        def _():
            _load_expert_weights([(w1_hbm.at[layer, e], w1_ref), (w3_hbm.at[layer, e], w3_ref)],
                                 stage_ref, wsem, stage_rows)

        gather(b, slot, True, 8)
        xs = xg_ref[slot]
        xa_ref[...] = lax.bitcast_convert_type(xs & jnp.uint32(0xFFFF0000), F32).astype(BF16)
        xb_ref[...] = lax.bitcast_convert_type(xs << 16, F32).astype(BF16)
        nxt = jnp.minimum(b + 1, nu - 1)
        gather(nxt, 1 - slot, False, True)
        xa, xb = xa_ref[...], xb_ref[...]
        h1 = (jnp.dot(xa, w1_ref[0:half, :], preferred_element_type=F32)
              + jnp.dot(xb, w1_ref[half:2 * half, :], preferred_element_type=F32))
        h3 = (jnp.dot(xa, w3_ref[0:half, :], preferred_element_type=F32)
              + jnp.dot(xb, w3_ref[half:2 * half, :], preferred_element_type=F32))
        o_ref[...] = (h1 * jax.nn.sigmoid(h1) * h3).astype(BF16)

        @pl.when(b == nu - 1)
        def _():
            gather(nxt, 1 - slot, True, 8)

    @pl.when(b >= nu)
    def _():
        o_ref[...] = jnp.zeros(o_ref.shape, BF16)


def _expert_up(blk_exp, n_used, src, hp, w1, w3, layer, bm, stage_rows):
    r = src.shape[0]
    half = hp.shape[1]
    d, fdim = w1.shape[2], w1.shape[3]
    any_spec = pl.BlockSpec(memory_space=pl.ANY)
    return pl.pallas_call(
        functools.partial(_expert_up_kernel, layer=layer, stage_rows=stage_rows),
        grid_spec=pltpu.PrefetchScalarGridSpec(
            num_scalar_prefetch=3,
            grid=(r // bm,),
            in_specs=[any_spec, any_spec, any_spec],
            out_specs=pl.BlockSpec((bm, fdim), lambda b, be, nu, src: (b, 0)),
            scratch_shapes=[pltpu.VMEM((2, bm, half), jnp.uint32),
                            pltpu.VMEM((bm, half), BF16),
                            pltpu.VMEM((bm, half), BF16),
                            pltpu.VMEM((d, fdim), BF16),
                            pltpu.VMEM((d, fdim), BF16),
                            pltpu.VMEM((2, stage_rows, fdim), F32),
                            pltpu.SemaphoreType.DMA((2,)),
                            pltpu.SemaphoreType.DMA((2,))]),
        out_shape=jax.ShapeDtypeStruct((r, fdim), BF16),
        compiler_params=_params(("arbitrary",), 56),
        name="expert_up",
    )(blk_exp, n_used, src, hp, w1, w3)


def _expert_down_kernel(be_ref, nu_ref, h_ref, w2_hbm, o_ref, w2_ref, stage_ref, wsem,
                        *, layer, stage_rows):
    b = pl.program_id(0)

    @pl.when(b < nu_ref[0])
    def _():
        @pl.when(_expert_changed(be_ref, b))
        def _():
            _load_expert_weights([(w2_hbm.at[layer, be_ref[b]], w2_ref)], stage_ref, wsem, stage_rows)

        o_ref[...] = jnp.dot(h_ref[...], w2_ref[...], preferred_element_type=F32)

    @pl.when(b >= nu_ref[0])
    def _():
        o_ref[...] = jnp.zeros(o_ref.shape, F32)


def _expert_down(blk_exp, n_used, hmid, w2, layer, bm, stage_rows):
    r, fdim = hmid.shape
    d = w2.shape[3]
    return pl.pallas_call(
        functools.partial(_expert_down_kernel, layer=layer, stage_rows=stage_rows),
        grid_spec=pltpu.PrefetchScalarGridSpec(
            num_scalar_prefetch=2,
            grid=(r // bm,),
            in_specs=[pl.BlockSpec((bm, fdim), lambda b, be, nu: (jnp.maximum(jnp.minimum(b, nu[0] - 1), 0), 0)),
                      pl.BlockSpec(memory_space=pl.ANY)],
            out_specs=pl.BlockSpec((bm, d), lambda b, be, nu: (b, 0)),
            scratch_shapes=[pltpu.VMEM((fdim, d), BF16),
                            pltpu.VMEM((2, stage_rows, d), F32),
                            pltpu.SemaphoreType.DMA((2,))]),
        out_shape=jax.ShapeDtypeStruct((r, d), F32),
        compiler_params=_params(("arbitrary",), 48),
        name="expert_down",
    )(blk_exp, n_used, hmid, w2)


def _combine_kernel(dest_ref, gate_ref, x_ref, p_ref, g_ref, wd_ref, wu_ref, wp_ref, y_hbm,
                    o_ref, ybuf_ref, sem):
    i = pl.program_id(0)
    n = pl.num_programs(0)
    tq = x_ref.shape[0]
    t_total = dest_ref.shape[0] // TOP_K
    slot = i % 2

    def gather(tile, slot_, wait, unroll):
        def body(t, c):
            for k in range(TOP_K):
                cp = _row_copy(y_hbm, dest_ref[k * t_total + tile * tq + t], ybuf_ref.at[slot_, k], t,
                               sem.at[slot_])
                if wait:
                    cp.wait()
                else:
                    cp.start()
            return c

        lax.fori_loop(0, tq, body, 0, unroll=unroll)

    @pl.when(i == 0)
    def _():
        gather(0, 0, False, 4)

    gather(i, slot, True, 4)
    moe = ybuf_ref[slot, 0] * gate_ref[:, 0:1] + ybuf_ref[slot, 1] * gate_ref[:, 1:2]
    x = x_ref[...] + moe
    nxt = jnp.minimum(i + 1, n - 1)
    gather(nxt, 1 - slot, False, True)
    hn = _rms(x, g_ref[...]).astype(BF16)
    low = jnp.dot(hn, wd_ref[...], preferred_element_type=F32).astype(BF16)
    gate = jax.nn.sigmoid(jnp.dot(low, wu_ref[...], preferred_element_type=F32))
    emb = jnp.dot(p_ref[...].astype(BF16), wp_ref[...], preferred_element_type=F32)
    o_ref[...] = x + gate * emb

    @pl.when(i == n - 1)
    def _():
        gather(nxt, 1 - slot, True, 4)


def _combine(dest, gates, x, p, g, wd, wu, wp, ybuf, tq):
    t, d = x.shape
    pd = p.shape[1]

    def full(a):
        return pl.BlockSpec(a.shape, lambda i, dest: (0, 0))

    return pl.pallas_call(
        _combine_kernel,
        grid_spec=pltpu.PrefetchScalarGridSpec(
            num_scalar_prefetch=1,
            grid=(t // tq,),
            in_specs=[pl.BlockSpec((tq, TOP_K), lambda i, dest: (i, 0)),
                      pl.BlockSpec((tq, d), lambda i, dest: (i, 0)),
                      pl.BlockSpec((tq, pd), lambda i, dest: (i, 0)),
                      full(g), full(wd), full(wu), full(wp),
                      pl.BlockSpec(memory_space=pl.ANY)],
            out_specs=pl.BlockSpec((tq, d), lambda i, dest: (i, 0)),
            scratch_shapes=[pltpu.VMEM((2, TOP_K, tq, d), F32), pltpu.SemaphoreType.DMA((2,))]),
        out_shape=jax.ShapeDtypeStruct((t, d), F32),
        input_output_aliases={2: 0},
        compiler_params=_params(("arbitrary",), 52),
        name="combine",
    )(dest, gates, x, p, g, wd, wu, wp, ybuf)


def _final_kernel(x_ref, g_ref, o_ref):
    o_ref[...] = _rms(x_ref[...], g_ref[...])


def _final_norm(x, g, row0, rows, tm):
    d = x.shape[1]
    rb0 = row0 // tm
    return pl.pallas_call(
        _final_kernel,
        grid=(rows // tm,),
        in_specs=[pl.BlockSpec((tm, d), lambda i: (rb0 + i, 0)),
                  pl.BlockSpec((1, d), lambda i: (0, 0))],
        out_specs=pl.BlockSpec((tm, d), lambda i: (i, 0)),
        out_shape=jax.ShapeDtypeStruct((rows, d), F32),
        compiler_params=_params(("parallel",), 32),
        name="final_norm",
    )(x, g)


def _tile(n, pref):
    t = min(n, pref)
    while n % t:
        t //= 2
    return t


def kernel(x_prompt, x_sample, state_pool, p_prompt, p_sample, g_mix, w_in, w_pool, pool_scale, sgu_ln_g, sgu_ln_b, sgu_w, sgu_b, w_out, g_ffn, router_grp_w, router_grp_b, router_exp_w, router_exp_b, moe_w1, moe_w3, moe_w2, g_ple, ple_gate_down, ple_gate_up, ple_proj, g_final):
    depth = w_in.shape[0]
    nb_p, seq_p, d = x_prompt.shape
    nb_s, seq_s, _ = x_sample.shape
    dp = state_pool.shape[-1]
    t_p, t_s = nb_p * seq_p, nb_s * seq_s
    t = t_p + t_s
    n_past_s = min(POOL_CACHE, PAST_LEN)

    tm = _tile(t, 512)
    tn_in = _tile(dp, 1024)
    tn_out = _tile(d, 1024)
    ts_p = _tile(seq_p, 256)
    tq = _tile(t, 256)
    bm = _tile(t * TOP_K, 512)
    stage_up = _tile(d, 512)
    stage_down = _tile(moe_w2.shape[2], 128)
    r_max = t * TOP_K + N_EXPERTS * bm
    nb_max = r_max // bm

    x = jnp.concatenate([x_prompt.reshape(t_p, d), x_sample.reshape(t_s, d)], axis=0)
    p_all = jnp.concatenate([p_prompt.reshape(depth, t_p, -1), p_sample.reshape(depth, t_s, -1)], axis=1)
    bf = lambda a: a.astype(BF16)
    n_log = MOE_GROUPS + N_EXPERTS
    wr = jnp.concatenate([router_grp_w, router_exp_w], axis=-1)
    wr = jnp.pad(jnp.swapaxes(wr, 1, 2), ((0, 0), (0, ROUTER_ROWS - n_log), (0, 0)))
    br = jnp.concatenate([router_grp_b, router_exp_b], axis=-1)
    br = jnp.broadcast_to(jnp.pad(br, ((0, 0), (0, ROUTER_ROWS - n_log)))[:, :, None],
                          (depth, ROUTER_ROWS, 128))
    bst = jnp.swapaxes(sgu_b, 1, 2)

    pool_p, pool_s, v_s = [], [], []
    for i in range(depth):
        row = lambda a: a[i][None, :]
        z = _inproj(x, row(g_mix), bf(w_in[i]), dp, tm, tn_in)
        mix_w = (bf(w_pool[i]), row(pool_scale), row(sgu_ln_g), row(sgu_ln_b), sgu_w[i], bst[i])
        m_p, np_i = _mixer(z, None, *mix_w, layer=i, row0=0, n_seq=nb_p, seq=seq_p, ts=ts_p,
                           n_past=0, emit_v=False)
        m_s, ns_i, vs_i = _mixer(z, state_pool, *mix_w, layer=i, row0=t_p, n_seq=nb_s, seq=seq_s,
                                 ts=seq_s, n_past=n_past_s, emit_v=True)
        pool_p.append(np_i)
        pool_s.append(ns_i)
        v_s.append(vs_i)
        x = _outproj(m_p, m_s, bf(w_out[i]), x, tm, tn_out)

        hp, ri, rf, cnt = _router(x, row(g_ffn), wr[i], br[i], tm)
        counts = cnt[:, 0].astype(jnp.int32)
        padded = (counts + bm - 1) // bm * bm
        pend = jnp.cumsum(padded)
        pstart = pend - padded
        n_used = (pend[-1:] // bm).astype(jnp.int32)
        blk_first = jnp.arange(nb_max, dtype=jnp.int32) * bm
        blk_exp = jnp.minimum(jnp.sum((pend[None, :] <= blk_first[:, None]).astype(jnp.int32), axis=1),
                              N_EXPERTS - 1)
        eid, rank = ri[0:TOP_K], ri[TOP_K:2 * TOP_K]
        onehot = eid[:, :, None] == jnp.arange(N_EXPERTS, dtype=jnp.int32)
        dest = (jnp.sum(jnp.where(onehot, pstart, 0), axis=-1) + rank).reshape(-1)
        src = _route_index(dest, r_max, tm)
        gates = jnp.transpose(rf[0:TOP_K])

        hmid = _expert_up(blk_exp, n_used, src, hp, moe_w1, moe_w3, i, bm, stage_up)
        ybuf = _expert_down(blk_exp, n_used, hmid, moe_w2, i, bm, stage_down)
        x = _combine(dest, gates, x, p_all[i], row(g_ple), bf(ple_gate_down[i]), bf(ple_gate_up[i]),
                     bf(ple_proj[i]), ybuf, tq)

    g_fin = g_final[None, :]
    y_prompt = _final_norm(x, g_fin, 0, t_p, tq).reshape(nb_p, seq_p, d)
    y_sample = _final_norm(x, g_fin, t_p, t_s, tq).reshape(nb_s, seq_s, d)
    return (y_prompt, y_sample, jnp.stack(pool_p), jnp.stack(pool_s), jnp.stack(v_s))
```

```python
import functools

import numpy as np
import jax
import jax.numpy as jnp
from jax import lax
from jax.experimental import pallas as pl
from jax.experimental.pallas import tpu as pltpu

PAST_LEN = 2048
POOL_WINDOWS = (2, 4, 8, 16)
POOL_CACHE = max(POOL_WINDOWS) - 1
POOL_PAD = 32
POOL_HALO = 16
SGU_CHUNK = 128
SGU_HEADS = 8
MOE_GROUPS = 4
MOE_EXP_PER_GROUP = 4
N_EXPERTS = MOE_GROUPS * MOE_EXP_PER_GROUP
TOP_K = 2
EPS = 1e-6
ROUTER_ROWS = 32

F32 = jnp.float32
BF16 = jnp.bfloat16
MIB = 2 ** 20


def _params(semantics, vmem_mib):
    return pltpu.CompilerParams(dimension_semantics=semantics, vmem_limit_bytes=vmem_mib * MIB)


def _rms(x, g):
    return x * lax.rsqrt(jnp.mean(x * x, axis=-1, keepdims=True) + EPS) * g


def _gelu(x):
    return 0.5 * x * (1.0 + lax.erf(x * np.float32(np.sqrt(0.5))))


def _inproj_kernel(x_ref, g_ref, w_ref, z_ref, h_ref, *, n_raw):
    j = pl.program_id(1)

    @pl.when(j == 0)
    def _():
        h_ref[...] = _rms(x_ref[...], g_ref[...]).astype(BF16)

    acc = jnp.dot(h_ref[...], w_ref[...], preferred_element_type=F32)

    @pl.when(j < n_raw)
    def _():
        z_ref[...] = acc

    @pl.when(j >= n_raw)
    def _():
        z_ref[...] = _gelu(acc)


def _inproj(x, g, w, d_pool, tm, tn):
    t, d = x.shape
    n = w.shape[1]
    return pl.pallas_call(
        functools.partial(_inproj_kernel, n_raw=d_pool // tn),
        grid=(t // tm, n // tn),
        in_specs=[pl.BlockSpec((tm, d), lambda i, j: (i, 0)),
                  pl.BlockSpec((1, d), lambda i, j: (0, 0)),
                  pl.BlockSpec((d, tn), lambda i, j: (0, j))],
        out_specs=pl.BlockSpec((tm, tn), lambda i, j: (i, j)),
        out_shape=jax.ShapeDtypeStruct((t, n), F32),
        scratch_shapes=[pltpu.VMEM((tm, d), BF16)],
        compiler_params=_params(("parallel", "arbitrary"), 60),
        name="inproj",
    )(x, g, w)


def _mixer_kernel(*refs, ts, chunk, has_past, n_past, emit_v):
    it = iter(refs)
    za_ref, u_ref, v_ref = next(it), next(it), next(it)
    past_ref = next(it) if has_past else None
    wpool_ref, pscale_ref, lng_ref, lnb_ref, ws_ref, bst_ref = (next(it) for _ in range(6))
    m_ref, pool_ref = next(it), next(it)
    vout_ref = next(it) if emit_v else None
    ext_ref, lvl_a_ref, lvl_b_ref = next(it), next(it), next(it)

    s = pl.program_id(1)
    dp = za_ref.shape[1]
    gc = dp // len(POOL_WINDOWS)
    lo = POOL_PAD

    @pl.when(s == 0)
    def _():
        for ref in (lvl_a_ref, lvl_b_ref):
            ref[0:POOL_HALO, :] = jnp.zeros((POOL_HALO, gc), F32)
        if has_past:
            ext_ref[0:lo - POOL_CACHE, :] = jnp.zeros((lo - POOL_CACHE, dp), F32)
            ext_ref[lo - POOL_CACHE:lo, :] = past_ref[...]
        else:
            ext_ref[0:lo, :] = jnp.zeros((lo, dp), F32)

    za = za_ref[...]
    ext_ref[lo:lo + ts, :] = za
    pos = s * ts + lax.broadcasted_iota(jnp.int32, (ts, 1), 0)
    top = POOL_HALO
    for g, w in enumerate(POOL_WINDOWS):
        c0 = g * gc
        src, cols, k = ext_ref, slice(c0, c0 + gc), 1
        while 2 * k < w:
            dst = (lvl_a_ref, lvl_b_ref)[(k.bit_length() - 1) % 2]
            dst[top:lo + ts, :] = src[top:lo + ts, cols] + src[top - k:lo + ts - k, cols]
            src, cols, k = dst, slice(0, gc), 2 * k
        win = src[lo:lo + ts, cols] + src[lo - k:lo + ts - k, cols]
        zg = za[:, c0:c0 + gc]
        cnt = jnp.minimum(w, pos + 1 + n_past).astype(F32)
        dlt = win * (1.0 / cnt) - zg
        y = jnp.dot(dlt.astype(BF16), wpool_ref[g], preferred_element_type=F32)
        m_ref[:, c0:c0 + gc] = (y * pscale_ref[:, c0:c0 + gc]).astype(BF16)

    tail = za[ts - POOL_CACHE:ts, :]
    pool_ref[...] = tail
    ext_ref[lo - POOL_CACHE:lo, :] = tail

    v = v_ref[...]
    mu = jnp.mean(v, axis=-1, keepdims=True)
    vc = v - mu
    var = jnp.mean(vc * vc, axis=-1, keepdims=True)
    vln = vc * lax.rsqrt(var + EPS) * lng_ref[...] + lnb_ref[...]
    if emit_v:
        vout_ref[...] = vln
    vb = vln.astype(BF16)
    u = u_ref[...]
    hd = u.shape[1] // SGU_HEADS
    tril = (lax.broadcasted_iota(jnp.int32, (chunk, chunk), 0)
            >= lax.broadcasted_iota(jnp.int32, (chunk, chunk), 1))
    for h in range(SGU_HEADS):
        wm = jnp.where(tril, ws_ref[h, 0:chunk, 0:chunk], 0.0).astype(BF16)
        bias = bst_ref[0:chunk, h:h + 1]
        for c in range(ts // chunk):
            rows = slice(c * chunk, (c + 1) * chunk)
            cols = slice(h * hd, (h + 1) * hd)
            mixed = jnp.dot(wm, vb[rows, cols], preferred_element_type=F32) + bias
            m_ref[rows, dp + h * hd:dp + (h + 1) * hd] = (u[rows, cols] * mixed).astype(BF16)


def _mixer(z, past, wpool, pscale, lng, lnb, ws, bst, *, layer, row0, n_seq, seq, ts, n_past, emit_v):
    t = z.shape[0]
    dp = pscale.shape[1]
    ds = lng.shape[1]
    assert dp == ds and z.shape[1] == dp + 2 * ds
    d = dp + ds
    n_tiles = seq // ts
    chunk = min(seq, SGU_CHUNK)
    assert ts % chunk == 0 and row0 % ts == 0
    rb0 = row0 // ts
    has_past = past is not None

    def zspec(col):
        return pl.BlockSpec((ts, dp), lambda b, s: (rb0 + b * n_tiles + s, col))

    def full(a):
        nd = a.ndim
        return pl.BlockSpec(a.shape, lambda b, s: (0,) * nd)

    in_specs = [zspec(0), zspec(1), zspec(2)]
    args = [z, z, z]
    if has_past:
        in_specs.append(pl.BlockSpec((None, None, POOL_CACHE, dp), lambda b, s: (layer, b, 0, 0)))
        args.append(past)
    small = [wpool, pscale, lng, lnb, ws, bst]
    in_specs += [full(a) for a in small]
    args += small

    out_shape = [jax.ShapeDtypeStruct((n_seq * seq, d), BF16),
                 jax.ShapeDtypeStruct((n_seq, POOL_CACHE, dp), F32)]
    out_specs = [pl.BlockSpec((ts, d), lambda b, s: (b * n_tiles + s, 0)),
                 pl.BlockSpec((None, POOL_CACHE, dp), lambda b, s: (b, 0, 0))]
    if emit_v:
        out_shape.append(jax.ShapeDtypeStruct((n_seq, seq, ds), F32))
        out_specs.append(pl.BlockSpec((None, ts, ds), lambda b, s: (b, s, 0)))
    del t
    return pl.pallas_call(
        functools.partial(_mixer_kernel, ts=ts, chunk=chunk, has_past=has_past,
                          n_past=n_past, emit_v=emit_v),
        grid=(n_seq, n_tiles),
        in_specs=in_specs,
        out_specs=out_specs,
        out_shape=out_shape,
        scratch_shapes=[pltpu.VMEM((POOL_PAD + ts, dp), F32),
                        pltpu.VMEM((POOL_PAD + ts, dp // len(POOL_WINDOWS)), F32),
                        pltpu.VMEM((POOL_PAD + ts, dp // len(POOL_WINDOWS)), F32)],
        compiler_params=_params(("parallel", "arbitrary"), 48),
        name="mixer_v" if emit_v else "mixer",
    )(*args)


def _outproj_kernel(ma_ref, mb_ref, w_ref, x_ref, o_ref, *, tiles_a):
    i = pl.program_id(0)

    @pl.when(i < tiles_a)
    def _():
        o_ref[...] = x_ref[...] + jnp.dot(ma_ref[...], w_ref[...], preferred_element_type=F32)

    @pl.when(i >= tiles_a)
    def _():
        o_ref[...] = x_ref[...] + jnp.dot(mb_ref[...], w_ref[...], preferred_element_type=F32)


def _outproj(m_a, m_b, w, x, tm, tn):
    t, d = x.shape
    k = m_a.shape[1]
    tiles_a, tiles_b = m_a.shape[0] // tm, m_b.shape[0] // tm
    assert tiles_a * tm == m_a.shape[0] and tiles_b * tm == m_b.shape[0] and (tiles_a + tiles_b) * tm == t
    return pl.pallas_call(
        functools.partial(_outproj_kernel, tiles_a=tiles_a),
        grid=(t // tm, d // tn),
        in_specs=[pl.BlockSpec((tm, k), lambda i, j: (jnp.minimum(i, tiles_a - 1), 0)),
                  pl.BlockSpec((tm, k), lambda i, j: (jnp.maximum(i - tiles_a, 0), 0)),
                  pl.BlockSpec((k, tn), lambda i, j: (0, j)),
                  pl.BlockSpec((tm, tn), lambda i, j: (i, j))],
        out_specs=pl.BlockSpec((tm, tn), lambda i, j: (i, j)),
        out_shape=jax.ShapeDtypeStruct((t, d), F32),
        input_output_aliases={3: 0},
        compiler_params=_params(("parallel", "arbitrary"), 52),
        name="outproj",
    )(m_a, m_b, w, x)


def _first_max(vals):
    mx = vals[0]
    for v in vals[1:]:
        mx = jnp.maximum(mx, v)
    idx = jnp.full(mx.shape, len(vals) - 1, jnp.int32)
    for j in range(len(vals) - 2, -1, -1):
        idx = jnp.where(vals[j] == mx, j, idx)
    return mx, idx


def _router_kernel(x_ref, g_ref, whi_ref, wlo_ref, b_ref, h_ref, ri_ref, rf_ref, cnt_ref, carry_ref):
    i = pl.program_id(0)
    tm = x_ref.shape[0]

    @pl.when(i == 0)
    def _():
        carry_ref[...] = jnp.zeros(carry_ref.shape, F32)

    h = _rms(x_ref[...], g_ref[...])
    hb = h.astype(BF16)
    half = h.shape[1] // 2
    hi = lax.bitcast_convert_type(hb[:, :half].astype(F32), jnp.uint32)
    lo = lax.bitcast_convert_type(hb[:, half:].astype(F32), jnp.uint32)
    h_ref[...] = hi | (lo >> 16)
    h_lo = (h - hb.astype(F32)).astype(BF16)
    nt = (((1,), (1,)), ((), ()))
    logits = (lax.dot_general(whi_ref[...], hb, nt, preferred_element_type=F32)
              + lax.dot_general(whi_ref[...], h_lo, nt, preferred_element_type=F32)
              + lax.dot_general(wlo_ref[...], hb, nt, preferred_element_type=F32)) + b_ref[:, 0:1]
    row = [logits[j:j + 1, :] for j in range(MOE_GROUPS + N_EXPERTS)]
    lg = row[:MOE_GROUPS]
    gmax, grp = _first_max(lg)
    gsum = jnp.exp(lg[0] - gmax)
    for j in range(1, MOE_GROUPS):
        gsum = gsum + jnp.exp(lg[j] - gmax)
    pg = 1.0 / gsum
    ev = []
    for j in range(MOE_EXP_PER_GROUP):
        sel = row[MOE_GROUPS + (MOE_GROUPS - 1) * MOE_EXP_PER_GROUP + j]
        for gi in range(MOE_GROUPS - 2, -1, -1):
            sel = jnp.where(grp == gi, row[MOE_GROUPS + gi * MOE_EXP_PER_GROUP + j], sel)
        ev.append(sel)
    v1, i1 = _first_max(ev)
    rest = [jnp.where(i1 == j, -jnp.inf, ev[j]) for j in range(MOE_EXP_PER_GROUP)]
    v2, i2 = _first_max(rest)
    e21 = jnp.exp(v2 - v1)
    den = 1.0 + e21
    gate0 = pg * (1.0 / den)
    gate1 = pg * (e21 / den)
    eid0 = grp * MOE_EXP_PER_GROUP + i1
    eid1 = grp * MOE_EXP_PER_GROUP + i2

    erow = lax.broadcasted_iota(jnp.int32, (N_EXPERTS, tm), 0)
    c0 = erow == eid0
    c1 = erow == eid1
    hit = jnp.logical_or(c0, c1)
    upper = (lax.broadcasted_iota(jnp.int32, (tm, tm), 0)
             < lax.broadcasted_iota(jnp.int32, (tm, tm), 1))
    before = jnp.dot(hit.astype(BF16), upper.astype(BF16), preferred_element_type=F32)
    before = before + carry_ref[:, 0:1]
    r0 = jnp.sum(jnp.where(c0, before, 0.0), axis=0, keepdims=True)
    r1 = jnp.sum(jnp.where(c1, before, 0.0), axis=0, keepdims=True)
    carry_ref[...] = carry_ref[...] + jnp.sum(hit.astype(F32), axis=1, keepdims=True)
    cnt_ref[...] = carry_ref[...]

    ri_ref[...] = jnp.zeros(ri_ref.shape, jnp.int32)
    ri_ref[0:1, :] = eid0
    ri_ref[1:2, :] = eid1
    ri_ref[2:3, :] = r0.astype(jnp.int32)
    ri_ref[3:4, :] = r1.astype(jnp.int32)
    rf_ref[...] = jnp.zeros(rf_ref.shape, F32)
    rf_ref[0:1, :] = gate0
    rf_ref[1:2, :] = gate1


def _router(x, g, wr_hi, wr_lo, br, tm):
    t, d = x.shape
    return pl.pallas_call(
        _router_kernel,
        grid=(t // tm,),
        in_specs=[pl.BlockSpec((tm, d), lambda i: (i, 0)),
                  pl.BlockSpec((1, d), lambda i: (0, 0)),
                  pl.BlockSpec((ROUTER_ROWS, d), lambda i: (0, 0)),
                  pl.BlockSpec((ROUTER_ROWS, d), lambda i: (0, 0)),
                  pl.BlockSpec((ROUTER_ROWS, 128), lambda i: (0, 0))],
        out_specs=[pl.BlockSpec((tm, d // 2), lambda i: (i, 0)),
                   pl.BlockSpec((8, tm), lambda i: (0, i)),
                   pl.BlockSpec((8, tm), lambda i: (0, i)),
                   pl.BlockSpec((N_EXPERTS, 128), lambda i: (0, 0))],
        out_shape=[jax.ShapeDtypeStruct((t, d // 2), jnp.uint32),
                   jax.ShapeDtypeStruct((8, t), jnp.int32),
                   jax.ShapeDtypeStruct((8, t), F32),
                   jax.ShapeDtypeStruct((N_EXPERTS, 128), F32)],
        scratch_shapes=[pltpu.VMEM((N_EXPERTS, 128), F32)],
        compiler_params=_params(("arbitrary",), 56),
        name="router",
    )(x, g, wr_hi, wr_lo, br)


def _row_copy(src_hbm, src_row, dst, dst_row, sem):
    return pltpu.make_async_copy(src_hbm.at[pl.ds(src_row, 1), :], dst.at[pl.ds(dst_row, 1), :], sem)


def _route_index_kernel(dest_ref, src_ref, *, clear_steps, clear_rows, place_rows):
    i = pl.program_id(0)
    last = src_ref.shape[0] - 1
    t_total = dest_ref.shape[0] // TOP_K

    @pl.when(i < clear_steps)
    def _():
        def clear(r, c):
            src_ref[jnp.minimum(i * clear_rows + r, last)] = 0
            return c

        lax.fori_loop(0, clear_rows, clear, 0, unroll=8)

    @pl.when(i >= clear_steps)
    def _():
        base = (i - clear_steps) * place_rows

        def place(j, c):
            t = base + j
            for k in range(TOP_K):
                src_ref[dest_ref[k * t_total + t]] = t
            return c

        lax.fori_loop(0, place_rows, place, 0, unroll=8)


def _route_index(dest, r_max, place_rows):
    t = dest.shape[0] // TOP_K
    clear_rows = 4 * place_rows
    clear_steps = -(-r_max // clear_rows)
    smem = pl.BlockSpec(memory_space=pltpu.SMEM)
    return pl.pallas_call(
        functools.partial(_route_index_kernel, clear_steps=clear_steps, clear_rows=clear_rows,
                          place_rows=place_rows),
        grid=(clear_steps + t // place_rows,),
        in_specs=[smem],
        out_specs=smem,
        out_shape=jax.ShapeDtypeStruct((r_max,), jnp.int32),
        compiler_params=pltpu.CompilerParams(dimension_semantics=("arbitrary",)),
        name="route_index",
    )(dest)


def _load_expert_weights(pairs, stage_ref, sem, rows):
    jobs = [(w, res, c) for w, res in pairs for c in range(w.shape[0] // rows)]

    def copy(i):
        w, _, c = jobs[i]
        return pltpu.make_async_copy(w.at[pl.ds(c * rows, rows), :], stage_ref.at[i % 2], sem.at[i % 2])

    copy(0).start()
    for i, (_, res, c) in enumerate(jobs):
        if i + 1 < len(jobs):
            copy(i + 1).start()
        copy(i).wait()
        res[c * rows:(c + 1) * rows, :] = stage_ref[i % 2].astype(BF16)


def _expert_changed(be_ref, b):
    return jnp.logical_or(b == 0, be_ref[b] != be_ref[jnp.maximum(b - 1, 0)])


def _expert_up_kernel(be_ref, nu_ref, src_ref, hp_hbm, w1_hbm, w3_hbm, o_ref,
                      xg_ref, xa_ref, xb_ref, w1_ref, w3_ref, stage_ref, gsem, wsem,
                      *, layer, stage_rows):
    b = pl.program_id(0)
    nu = nu_ref[0]
    bm = o_ref.shape[0]
    half = xg_ref.shape[2]

    def gather(blk, slot, wait, unroll):
        def body(r, c):
            cp = _row_copy(hp_hbm, src_ref[blk * bm + r], xg_ref.at[slot], r, gsem.at[slot])
            if wait:
                cp.wait()
            else:
                cp.start()
            return c

        lax.fori_loop(0, bm, body, 0, unroll=unroll)

    @pl.when(b < nu)
    def _():
        e = be_ref[b]
        slot = b % 2

        @pl.when(b == 0)
        def _():
            gather(0, 0, False, 8)

        @pl.when(_expert_changed(be_ref, b))
        def _():
            _load_expert_weights([(w1_hbm.at[layer, e], w1_ref), (w3_hbm.at[layer, e], w3_ref)],
                                 stage_ref, wsem, stage_rows)

        gather(b, slot, True, 8)
        xs = xg_ref[slot]
        xa_ref[...] = lax.bitcast_convert_type(xs & jnp.uint32(0xFFFF0000), F32).astype(BF16)
        xb_ref[...] = lax.bitcast_convert_type(xs << 16, F32).astype(BF16)
        nxt = jnp.minimum(b + 1, nu - 1)
        gather(nxt, 1 - slot, False, True)
        xa, xb = xa_ref[...], xb_ref[...]
        h1 = (jnp.dot(xa, w1_ref[0:half, :], preferred_element_type=F32)
              + jnp.dot(xb, w1_ref[half:2 * half, :], preferred_element_type=F32))
        h3 = (jnp.dot(xa, w3_ref[0:half, :], preferred_element_type=F32)
              + jnp.dot(xb, w3_ref[half:2 * half, :], preferred_element_type=F32))
        o_ref[...] = (h1 * jax.nn.sigmoid(h1) * h3).astype(BF16)

        @pl.when(b == nu - 1)
        def _():
            gather(nxt, 1 - slot, True, 8)

    @pl.when(b >= nu)
    def _():
        o_ref[...] = jnp.zeros(o_ref.shape, BF16)


def _expert_up(blk_exp, n_used, src, hp, w1, w3, layer, bm, stage_rows):
    r = src.shape[0]
    half = hp.shape[1]
    d, fdim = w1.shape[2], w1.shape[3]
    any_spec = pl.BlockSpec(memory_space=pl.ANY)
    return pl.pallas_call(
        functools.partial(_expert_up_kernel, layer=layer, stage_rows=stage_rows),
        grid_spec=pltpu.PrefetchScalarGridSpec(
            num_scalar_prefetch=3,
            grid=(r // bm,),
            in_specs=[any_spec, any_spec, any_spec],
            out_specs=pl.BlockSpec((bm, fdim), lambda b, be, nu, src: (b, 0)),
            scratch_shapes=[pltpu.VMEM((2, bm, half), jnp.uint32),
                            pltpu.VMEM((bm, half), BF16),
                            pltpu.VMEM((bm, half), BF16),
                            pltpu.VMEM((d, fdim), BF16),
                            pltpu.VMEM((d, fdim), BF16),
                            pltpu.VMEM((2, stage_rows, fdim), F32),
                            pltpu.SemaphoreType.DMA((2,)),
                            pltpu.SemaphoreType.DMA((2,))]),
        out_shape=jax.ShapeDtypeStruct((r, fdim), BF16),
        compiler_params=_params(("arbitrary",), 56),
        name="expert_up",
    )(blk_exp, n_used, src, hp, w1, w3)


def _expert_down_kernel(be_ref, nu_ref, h_ref, w2_hbm, o_ref, w2_ref, stage_ref, wsem,
                        *, layer, stage_rows):
    b = pl.program_id(0)

    @pl.when(b < nu_ref[0])
    def _():
        @pl.when(_expert_changed(be_ref, b))
        def _():
            _load_expert_weights([(w2_hbm.at[layer, be_ref[b]], w2_ref)], stage_ref, wsem, stage_rows)

        o_ref[...] = jnp.dot(h_ref[...], w2_ref[...], preferred_element_type=F32)

    @pl.when(b >= nu_ref[0])
    def _():
        o_ref[...] = jnp.zeros(o_ref.shape, F32)


def _expert_down(blk_exp, n_used, hmid, w2, layer, bm, stage_rows):
    r, fdim = hmid.shape
    d = w2.shape[3]
    return pl.pallas_call(
        functools.partial(_expert_down_kernel, layer=layer, stage_rows=stage_rows),
        grid_spec=pltpu.PrefetchScalarGridSpec(
            num_scalar_prefetch=2,
            grid=(r // bm,),
            in_specs=[pl.BlockSpec((bm, fdim), lambda b, be, nu: (jnp.maximum(jnp.minimum(b, nu[0] - 1), 0), 0)),
                      pl.BlockSpec(memory_space=pl.ANY)],
            out_specs=pl.BlockSpec((bm, d), lambda b, be, nu: (b, 0)),
            scratch_shapes=[pltpu.VMEM((fdim, d), BF16),
                            pltpu.VMEM((2, stage_rows, d), F32),
                            pltpu.SemaphoreType.DMA((2,))]),
        out_shape=jax.ShapeDtypeStruct((r, d), F32),
        compiler_params=_params(("arbitrary",), 48),
        name="expert_down",
    )(blk_exp, n_used, hmid, w2)


def _combine_kernel(dest_ref, gate_ref, x_ref, p_ref, g_ref, wd_ref, wu_ref, wp_ref, y_hbm,
                    o_ref, ybuf_ref, sem):
    i = pl.program_id(0)
    n = pl.num_programs(0)
    tq = x_ref.shape[0]
    t_total = dest_ref.shape[0] // TOP_K
    slot = i % 2

    def gather(tile, slot_, wait, unroll):
        def body(t, c):
            for k in range(TOP_K):
                cp = _row_copy(y_hbm, dest_ref[k * t_total + tile * tq + t], ybuf_ref.at[slot_, k], t,
                               sem.at[slot_])
                if wait:
                    cp.wait()
                else:
                    cp.start()
            return c

        lax.fori_loop(0, tq, body, 0, unroll=unroll)

    @pl.when(i == 0)
    def _():
        gather(0, 0, False, 4)

    gather(i, slot, True, 4)
    n_sub = 2
    sub = tq // n_sub
    xs = []
    for h in range(n_sub):
        rows = slice(h * sub, (h + 1) * sub)
        moe = (ybuf_ref[slot, 0, rows, :] * gate_ref[rows, 0:1]
               + ybuf_ref[slot, 1, rows, :] * gate_ref[rows, 1:2])
        xs.append(x_ref[rows, :] + moe)
    nxt = jnp.minimum(i + 1, n - 1)
    gather(nxt, 1 - slot, False, True)
    for h in range(n_sub):
        rows = slice(h * sub, (h + 1) * sub)
        x = xs[h]
        hn = _rms(x, g_ref[...]).astype(BF16)
        low = jnp.dot(hn, wd_ref[...], preferred_element_type=F32).astype(BF16)
        gate = jax.nn.sigmoid(jnp.dot(low, wu_ref[...], preferred_element_type=F32))
        emb = jnp.dot(p_ref[rows, :].astype(BF16), wp_ref[...], preferred_element_type=F32)
        o_ref[rows, :] = x + gate * emb

    @pl.when(i == n - 1)
    def _():
        gather(nxt, 1 - slot, True, 4)


def _combine(dest, gates, x, p, g, wd, wu, wp, ybuf, tq):
    t, d = x.shape
    pd = p.shape[1]

    def full(a):
        return pl.BlockSpec(a.shape, lambda i, dest: (0, 0))

    return pl.pallas_call(
        _combine_kernel,
        grid_spec=pltpu.PrefetchScalarGridSpec(
            num_scalar_prefetch=1,
            grid=(t // tq,),
            in_specs=[pl.BlockSpec((tq, TOP_K), lambda i, dest: (i, 0)),
                      pl.BlockSpec((tq, d), lambda i, dest: (i, 0)),
                      pl.BlockSpec((tq, pd), lambda i, dest: (i, 0)),
                      full(g), full(wd), full(wu), full(wp),
                      pl.BlockSpec(memory_space=pl.ANY)],
            out_specs=pl.BlockSpec((tq, d), lambda i, dest: (i, 0)),
            scratch_shapes=[pltpu.VMEM((2, TOP_K, tq, d), F32), pltpu.SemaphoreType.DMA((2,))]),
        out_shape=jax.ShapeDtypeStruct((t, d), F32),
        input_output_aliases={2: 0},
        compiler_params=_params(("arbitrary",), 52),
        name="combine",
    )(dest, gates, x, p, g, wd, wu, wp, ybuf)


def _final_kernel(x_ref, g_ref, o_ref):
    o_ref[...] = _rms(x_ref[...], g_ref[...])


def _final_norm(x, g, row0, rows, tm):
    d = x.shape[1]
    rb0 = row0 // tm
    return pl.pallas_call(
        _final_kernel,
        grid=(rows // tm,),
        in_specs=[pl.BlockSpec((tm, d), lambda i: (rb0 + i, 0)),
                  pl.BlockSpec((1, d), lambda i: (0, 0))],
        out_specs=pl.BlockSpec((tm, d), lambda i: (i, 0)),
        out_shape=jax.ShapeDtypeStruct((rows, d), F32),
        compiler_params=_params(("parallel",), 32),
        name="final_norm",
    )(x, g)


def _tile(n, pref):
    t = min(n, pref)
    while n % t:
        t //= 2
    return t


def kernel(x_prompt, x_sample, state_pool, p_prompt, p_sample, g_mix, w_in, w_pool, pool_scale, sgu_ln_g, sgu_ln_b, sgu_w, sgu_b, w_out, g_ffn, router_grp_w, router_grp_b, router_exp_w, router_exp_b, moe_w1, moe_w3, moe_w2, g_ple, ple_gate_down, ple_gate_up, ple_proj, g_final):
    depth = w_in.shape[0]
    nb_p, seq_p, d = x_prompt.shape
    nb_s, seq_s, _ = x_sample.shape
    dp = state_pool.shape[-1]
    t_p, t_s = nb_p * seq_p, nb_s * seq_s
    t = t_p + t_s
    n_past_s = min(POOL_CACHE, PAST_LEN)

    tm = _tile(t, 512)
    tn_in = _tile(dp, 1024)
    tn_out = _tile(d, 1024)
    ts_p = _tile(seq_p, 256)
    tq = _tile(t, 256)
    bm = _tile(t * TOP_K, 512)
    stage_up = _tile(d, 512)
    stage_down = _tile(moe_w2.shape[2], 128)
    r_max = t * TOP_K + N_EXPERTS * bm
    nb_max = r_max // bm

    x = jnp.concatenate([x_prompt.reshape(t_p, d), x_sample.reshape(t_s, d)], axis=0)
    p_all = jnp.concatenate([p_prompt.reshape(depth, t_p, -1), p_sample.reshape(depth, t_s, -1)], axis=1)
    bf = lambda a: a.astype(BF16)
    n_log = MOE_GROUPS + N_EXPERTS
    wr = jnp.concatenate([router_grp_w, router_exp_w], axis=-1)
    wr = jnp.pad(jnp.swapaxes(wr, 1, 2), ((0, 0), (0, ROUTER_ROWS - n_log), (0, 0)))
    wr_hi = wr.astype(BF16)
    wr_lo = (wr - wr_hi.astype(F32)).astype(BF16)
    br = jnp.concatenate([router_grp_b, router_exp_b], axis=-1)
    br = jnp.broadcast_to(jnp.pad(br, ((0, 0), (0, ROUTER_ROWS - n_log)))[:, :, None],
                          (depth, ROUTER_ROWS, 128))
    bst = jnp.swapaxes(sgu_b, 1, 2)

    pool_p, pool_s, v_s = [], [], []
    for i in range(depth):
        row = lambda a: a[i][None, :]
        z = _inproj(x, row(g_mix), bf(w_in[i]), dp, tm, tn_in)
        mix_w = (bf(w_pool[i]), row(pool_scale), row(sgu_ln_g), row(sgu_ln_b), sgu_w[i], bst[i])
        m_p, np_i = _mixer(z, None, *mix_w, layer=i, row0=0, n_seq=nb_p, seq=seq_p, ts=ts_p,
                           n_past=0, emit_v=False)
        m_s, ns_i, vs_i = _mixer(z, state_pool, *mix_w, layer=i, row0=t_p, n_seq=nb_s, seq=seq_s,
                                 ts=seq_s, n_past=n_past_s, emit_v=True)
        pool_p.append(np_i)
        pool_s.append(ns_i)
        v_s.append(vs_i)
        x = _outproj(m_p, m_s, bf(w_out[i]), x, tm, tn_out)

        hp, ri, rf, cnt = _router(x, row(g_ffn), wr_hi[i], wr_lo[i], br[i], tm)
        counts = cnt[:, 0].astype(jnp.int32)
        padded = (counts + bm - 1) // bm * bm
        pend = jnp.cumsum(padded)
        pstart = pend - padded
        n_used = (pend[-1:] // bm).astype(jnp.int32)
        blk_first = jnp.arange(nb_max, dtype=jnp.int32) * bm
        blk_exp = jnp.minimum(jnp.sum((pend[None, :] <= blk_first[:, None]).astype(jnp.int32), axis=1),
                              N_EXPERTS - 1)
        eid, rank = ri[0:TOP_K], ri[TOP_K:2 * TOP_K]
        onehot = eid[:, :, None] == jnp.arange(N_EXPERTS, dtype=jnp.int32)
        dest = (jnp.sum(jnp.where(onehot, pstart, 0), axis=-1) + rank).reshape(-1)
        src = _route_index(dest, r_max, tm)
        gates = jnp.transpose(rf[0:TOP_K])

        hmid = _expert_up(blk_exp, n_used, src, hp, moe_w1, moe_w3, i, bm, stage_up)
        ybuf = _expert_down(blk_exp, n_used, hmid, moe_w2, i, bm, stage_down)
        x = _combine(dest, gates, x, p_all[i], row(g_ple), bf(ple_gate_down[i]), bf(ple_gate_up[i]),
                     bf(ple_proj[i]), ybuf, tq)

    g_fin = g_final[None, :]
    y_prompt = _final_norm(x, g_fin, 0, t_p, tq).reshape(nb_p, seq_p, d)
    y_sample = _final_norm(x, g_fin, t_p, t_s, tq).reshape(nb_s, seq_s, d)
    return (y_prompt, y_sample, jnp.stack(pool_p), jnp.stack(pool_s), jnp.stack(v_s))
```

```python
import functools

import numpy as np
import jax
import jax.numpy as jnp
from jax import lax
from jax.experimental import pallas as pl
from jax.experimental.pallas import tpu as pltpu

PAST_LEN = 2048
POOL_WINDOWS = (2, 4, 8, 16)
POOL_CACHE = max(POOL_WINDOWS) - 1
POOL_PAD = 32
POOL_HALO = 16
SGU_CHUNK = 128
SGU_HEADS = 8
MOE_GROUPS = 4
MOE_EXP_PER_GROUP = 4
N_EXPERTS = MOE_GROUPS * MOE_EXP_PER_GROUP
TOP_K = 2
EPS = 1e-6
ROUTER_ROWS = 32

F32 = jnp.float32
BF16 = jnp.bfloat16
MIB = 2 ** 20


def _params(semantics, vmem_mib):
    return pltpu.CompilerParams(dimension_semantics=semantics, vmem_limit_bytes=vmem_mib * MIB)


def _rms(x, g):
    return x * lax.rsqrt(jnp.mean(x * x, axis=-1, keepdims=True) + EPS) * g


def _gelu(x):
    return 0.5 * x * (1.0 + lax.erf(x * np.float32(np.sqrt(0.5))))


def _inproj_kernel(x_ref, g_ref, w_ref, z_ref, h_ref, *, n_raw):
    j = pl.program_id(1)

    @pl.when(j == 0)
    def _():
        h_ref[...] = _rms(x_ref[...], g_ref[...]).astype(BF16)

    acc = jnp.dot(h_ref[...], w_ref[...], preferred_element_type=F32)

    @pl.when(j < n_raw)
    def _():
        z_ref[...] = acc

    @pl.when(j >= n_raw)
    def _():
        z_ref[...] = _gelu(acc)


def _inproj(x, g, w, d_pool, tm, tn):
    t, d = x.shape
    n = w.shape[1]
    return pl.pallas_call(
        functools.partial(_inproj_kernel, n_raw=d_pool // tn),
        grid=(t // tm, n // tn),
        in_specs=[pl.BlockSpec((tm, d), lambda i, j: (i, 0)),
                  pl.BlockSpec((1, d), lambda i, j: (0, 0)),
                  pl.BlockSpec((d, tn), lambda i, j: (0, j))],
        out_specs=pl.BlockSpec((tm, tn), lambda i, j: (i, j)),
        out_shape=jax.ShapeDtypeStruct((t, n), F32),
        scratch_shapes=[pltpu.VMEM((tm, d), BF16)],
        compiler_params=_params(("parallel", "arbitrary"), 60),
        name="inproj",
    )(x, g, w)


def _mixer_kernel(*refs, ts, chunk, has_past, n_past, emit_v):
    it = iter(refs)
    za_ref, u_ref, v_ref = next(it), next(it), next(it)
    past_ref = next(it) if has_past else None
    wpool_ref, pscale_ref, lng_ref, lnb_ref, ws_ref, bst_ref = (next(it) for _ in range(6))
    m_ref, pool_ref = next(it), next(it)
    vout_ref = next(it) if emit_v else None
    ext_ref, lvl_a_ref, lvl_b_ref = next(it), next(it), next(it)

    s = pl.program_id(1)
    dp = za_ref.shape[1]
    gc = dp // len(POOL_WINDOWS)
    lo = POOL_PAD

    @pl.when(s == 0)
    def _():
        for ref in (lvl_a_ref, lvl_b_ref):
            ref[0:POOL_HALO, :] = jnp.zeros((POOL_HALO, gc), F32)
        if has_past:
            ext_ref[0:lo - POOL_CACHE, :] = jnp.zeros((lo - POOL_CACHE, dp), F32)
            ext_ref[lo - POOL_CACHE:lo, :] = past_ref[...]
        else:
            ext_ref[0:lo, :] = jnp.zeros((lo, dp), F32)

    za = za_ref[...]
    ext_ref[lo:lo + ts, :] = za
    pos = s * ts + lax.broadcasted_iota(jnp.int32, (ts, 1), 0)
    top = POOL_HALO
    for g, w in enumerate(POOL_WINDOWS):
        c0 = g * gc
        src, cols, k = ext_ref, slice(c0, c0 + gc), 1
        while 2 * k < w:
            dst = (lvl_a_ref, lvl_b_ref)[(k.bit_length() - 1) % 2]
            dst[top:lo + ts, :] = src[top:lo + ts, cols] + src[top - k:lo + ts - k, cols]
            src, cols, k = dst, slice(0, gc), 2 * k
        win = src[lo:lo + ts, cols] + src[lo - k:lo + ts - k, cols]
        zg = za[:, c0:c0 + gc]
        cnt = jnp.minimum(w, pos + 1 + n_past).astype(F32)
        dlt = win * (1.0 / cnt) - zg
        y = jnp.dot(dlt.astype(BF16), wpool_ref[g], preferred_element_type=F32)
        m_ref[:, c0:c0 + gc] = (y * pscale_ref[:, c0:c0 + gc]).astype(BF16)

    tail = za[ts - POOL_CACHE:ts, :]
    pool_ref[...] = tail
    ext_ref[lo - POOL_CACHE:lo, :] = tail

    v = v_ref[...]
    mu = jnp.mean(v, axis=-1, keepdims=True)
    vc = v - mu
    var = jnp.mean(vc * vc, axis=-1, keepdims=True)
    vln = vc * lax.rsqrt(var + EPS) * lng_ref[...] + lnb_ref[...]
    if emit_v:
        vout_ref[...] = vln
    vb = vln.astype(BF16)
    u = u_ref[...]
    hd = u.shape[1] // SGU_HEADS
    tril = (lax.broadcasted_iota(jnp.int32, (chunk, chunk), 0)
            >= lax.broadcasted_iota(jnp.int32, (chunk, chunk), 1))
    for h in range(SGU_HEADS):
        wm = jnp.where(tril, ws_ref[h, 0:chunk, 0:chunk], 0.0).astype(BF16)
        bias = bst_ref[0:chunk, h:h + 1]
        for c in range(ts // chunk):
            rows = slice(c * chunk, (c + 1) * chunk)
            cols = slice(h * hd, (h + 1) * hd)
            mixed = jnp.dot(wm, vb[rows, cols], preferred_element_type=F32) + bias
            m_ref[rows, dp + h * hd:dp + (h + 1) * hd] = (u[rows, cols] * mixed).astype(BF16)


def _mixer(z, past, wpool, pscale, lng, lnb, ws, bst, *, layer, row0, n_seq, seq, ts, n_past, emit_v):
    t = z.shape[0]
    dp = pscale.shape[1]
    ds = lng.shape[1]
    assert dp == ds and z.shape[1] == dp + 2 * ds
    d = dp + ds
    n_tiles = seq // ts
    chunk = min(seq, SGU_CHUNK)
    assert ts % chunk == 0 and row0 % ts == 0
    rb0 = row0 // ts
    has_past = past is not None

    def zspec(col):
        return pl.BlockSpec((ts, dp), lambda b, s: (rb0 + b * n_tiles + s, col))

    def full(a):
        nd = a.ndim
        return pl.BlockSpec(a.shape, lambda b, s: (0,) * nd)

    in_specs = [zspec(0), zspec(1), zspec(2)]
    args = [z, z, z]
    if has_past:
        in_specs.append(pl.BlockSpec((None, None, POOL_CACHE, dp), lambda b, s: (layer, b, 0, 0)))
        args.append(past)
    small = [wpool, pscale, lng, lnb, ws, bst]
    in_specs += [full(a) for a in small]
    args += small

    out_shape = [jax.ShapeDtypeStruct((n_seq * seq, d), BF16),
                 jax.ShapeDtypeStruct((n_seq, POOL_CACHE, dp), F32)]
    out_specs = [pl.BlockSpec((ts, d), lambda b, s: (b * n_tiles + s, 0)),
                 pl.BlockSpec((None, POOL_CACHE, dp), lambda b, s: (b, 0, 0))]
    if emit_v:
        out_shape.append(jax.ShapeDtypeStruct((n_seq, seq, ds), F32))
        out_specs.append(pl.BlockSpec((None, ts, ds), lambda b, s: (b, s, 0)))
    del t
    return pl.pallas_call(
        functools.partial(_mixer_kernel, ts=ts, chunk=chunk, has_past=has_past,
                          n_past=n_past, emit_v=emit_v),
        grid=(n_seq, n_tiles),
        in_specs=in_specs,
        out_specs=out_specs,
        out_shape=out_shape,
        scratch_shapes=[pltpu.VMEM((POOL_PAD + ts, dp), F32),
                        pltpu.VMEM((POOL_PAD + ts, dp // len(POOL_WINDOWS)), F32),
                        pltpu.VMEM((POOL_PAD + ts, dp // len(POOL_WINDOWS)), F32)],
        compiler_params=_params(("parallel", "arbitrary"), 48),
        name="mixer_v" if emit_v else "mixer",
    )(*args)


def _outproj_kernel(ma_ref, mb_ref, w_ref, x_ref, o_ref, *, tiles_a):
    i = pl.program_id(0)

    @pl.when(i < tiles_a)
    def _():
        o_ref[...] = x_ref[...] + jnp.dot(ma_ref[...], w_ref[...], preferred_element_type=F32)

    @pl.when(i >= tiles_a)
    def _():
        o_ref[...] = x_ref[...] + jnp.dot(mb_ref[...], w_ref[...], preferred_element_type=F32)


def _outproj(m_a, m_b, w, x, tm, tn):
    t, d = x.shape
    k = m_a.shape[1]
    tiles_a, tiles_b = m_a.shape[0] // tm, m_b.shape[0] // tm
    assert tiles_a * tm == m_a.shape[0] and tiles_b * tm == m_b.shape[0] and (tiles_a + tiles_b) * tm == t
    return pl.pallas_call(
        functools.partial(_outproj_kernel, tiles_a=tiles_a),
        grid=(t // tm, d // tn),
        in_specs=[pl.BlockSpec((tm, k), lambda i, j: (jnp.minimum(i, tiles_a - 1), 0)),
                  pl.BlockSpec((tm, k), lambda i, j: (jnp.maximum(i - tiles_a, 0), 0)),
                  pl.BlockSpec((k, tn), lambda i, j: (0, j)),
                  pl.BlockSpec((tm, tn), lambda i, j: (i, j))],
        out_specs=pl.BlockSpec((tm, tn), lambda i, j: (i, j)),
        out_shape=jax.ShapeDtypeStruct((t, d), F32),
        input_output_aliases={3: 0},
        compiler_params=_params(("parallel", "arbitrary"), 52),
        name="outproj",
    )(m_a, m_b, w, x)


def _first_max(vals):
    mx = vals[0]
    for v in vals[1:]:
        mx = jnp.maximum(mx, v)
    idx = jnp.full(mx.shape, len(vals) - 1, jnp.int32)
    for j in range(len(vals) - 2, -1, -1):
        idx = jnp.where(vals[j] == mx, j, idx)
    return mx, idx


def _router_kernel(x_ref, g_ref, whi_ref, wlo_ref, b_ref, h_ref, ri_ref, rf_ref, cnt_ref, carry_ref):
    i = pl.program_id(0)
    tm = x_ref.shape[0]

    @pl.when(i == 0)
    def _():
        carry_ref[...] = jnp.zeros(carry_ref.shape, F32)

    h = _rms(x_ref[...], g_ref[...])
    hb = h.astype(BF16)
    half = h.shape[1] // 2
    hi = lax.bitcast_convert_type(hb[:, :half].astype(F32), jnp.uint32)
    lo = lax.bitcast_convert_type(hb[:, half:].astype(F32), jnp.uint32)
    h_ref[...] = hi | (lo >> 16)
    h_lo = (h - hb.astype(F32)).astype(BF16)
    nt = (((1,), (1,)), ((), ()))
    logits = (lax.dot_general(whi_ref[...], hb, nt, preferred_element_type=F32)
              + lax.dot_general(whi_ref[...], h_lo, nt, preferred_element_type=F32)
              + lax.dot_general(wlo_ref[...], hb, nt, preferred_element_type=F32)) + b_ref[:, 0:1]
    row = [logits[j:j + 1, :] for j in range(MOE_GROUPS + N_EXPERTS)]
    lg = row[:MOE_GROUPS]
    gmax, grp = _first_max(lg)
    gsum = jnp.exp(lg[0] - gmax)
    for j in range(1, MOE_GROUPS):
        gsum = gsum + jnp.exp(lg[j] - gmax)
    pg = 1.0 / gsum
    ev = []
    for j in range(MOE_EXP_PER_GROUP):
        sel = row[MOE_GROUPS + (MOE_GROUPS - 1) * MOE_EXP_PER_GROUP + j]
        for gi in range(MOE_GROUPS - 2, -1, -1):
            sel = jnp.where(grp == gi, row[MOE_GROUPS + gi * MOE_EXP_PER_GROUP + j], sel)
        ev.append(sel)
    v1, i1 = _first_max(ev)
    rest = [jnp.where(i1 == j, -jnp.inf, ev[j]) for j in range(MOE_EXP_PER_GROUP)]
    v2, i2 = _first_max(rest)
    e21 = jnp.exp(v2 - v1)
    den = 1.0 + e21
    gate0 = pg * (1.0 / den)
    gate1 = pg * (e21 / den)
    eid0 = grp * MOE_EXP_PER_GROUP + i1
    eid1 = grp * MOE_EXP_PER_GROUP + i2

    erow = lax.broadcasted_iota(jnp.int32, (N_EXPERTS, tm), 0)
    c0 = erow == eid0
    c1 = erow == eid1
    hit = jnp.logical_or(c0, c1)
    upper = (lax.broadcasted_iota(jnp.int32, (tm, tm), 0)
             < lax.broadcasted_iota(jnp.int32, (tm, tm), 1))
    before = jnp.dot(hit.astype(BF16), upper.astype(BF16), preferred_element_type=F32)
    before = before + carry_ref[:, 0:1]
    r0 = jnp.sum(jnp.where(c0, before, 0.0), axis=0, keepdims=True)
    r1 = jnp.sum(jnp.where(c1, before, 0.0), axis=0, keepdims=True)
    carry_ref[...] = carry_ref[...] + jnp.sum(hit.astype(F32), axis=1, keepdims=True)
    cnt_ref[...] = carry_ref[...]

    ri_ref[...] = jnp.zeros(ri_ref.shape, jnp.int32)
    ri_ref[0:1, :] = eid0
    ri_ref[1:2, :] = eid1
    ri_ref[2:3, :] = r0.astype(jnp.int32)
    ri_ref[3:4, :] = r1.astype(jnp.int32)
    rf_ref[...] = jnp.zeros(rf_ref.shape, F32)
    rf_ref[0:1, :] = gate0
    rf_ref[1:2, :] = gate1


def _router(x, g, wr_hi, wr_lo, br, tm):
    t, d = x.shape
    return pl.pallas_call(
        _router_kernel,
        grid=(t // tm,),
        in_specs=[pl.BlockSpec((tm, d), lambda i: (i, 0)),
                  pl.BlockSpec((1, d), lambda i: (0, 0)),
                  pl.BlockSpec((ROUTER_ROWS, d), lambda i: (0, 0)),
                  pl.BlockSpec((ROUTER_ROWS, d), lambda i: (0, 0)),
                  pl.BlockSpec((ROUTER_ROWS, 128), lambda i: (0, 0))],
        out_specs=[pl.BlockSpec((tm, d // 2), lambda i: (i, 0)),
                   pl.BlockSpec((8, tm), lambda i: (0, i)),
                   pl.BlockSpec((8, tm), lambda i: (0, i)),
                   pl.BlockSpec((N_EXPERTS, 128), lambda i: (0, 0))],
        out_shape=[jax.ShapeDtypeStruct((t, d // 2), jnp.uint32),
                   jax.ShapeDtypeStruct((8, t), jnp.int32),
                   jax.ShapeDtypeStruct((8, t), F32),
                   jax.ShapeDtypeStruct((N_EXPERTS, 128), F32)],
        scratch_shapes=[pltpu.VMEM((N_EXPERTS, 128), F32)],
        compiler_params=_params(("arbitrary",), 56),
        name="router",
    )(x, g, wr_hi, wr_lo, br)


def _row_copy(src_hbm, src_row, dst, dst_row, sem):
    return pltpu.make_async_copy(src_hbm.at[pl.ds(src_row, 1), :], dst.at[pl.ds(dst_row, 1), :], sem)


def _route_index_kernel(dest_ref, zeros_hbm, src_ref, sem, *, place_rows):
    i = pl.program_id(0)
    t_total = dest_ref.shape[0] // TOP_K

    @pl.when(i == 0)
    def _():
        fill = pltpu.make_async_copy(zeros_hbm, src_ref, sem)
        fill.start()
        fill.wait()

    base = i * place_rows

    def place(j, c):
        t = base + j
        for k in range(TOP_K):
            src_ref[dest_ref[k * t_total + t]] = t
        return c

    lax.fori_loop(0, place_rows, place, 0, unroll=8)


def _route_index(dest, r_max, place_rows):
    t = dest.shape[0] // TOP_K
    smem = pl.BlockSpec(memory_space=pltpu.SMEM)
    return pl.pallas_call(
        functools.partial(_route_index_kernel, place_rows=place_rows),
        grid=(t // place_rows,),
        in_specs=[smem, pl.BlockSpec(memory_space=pl.ANY)],
        out_specs=smem,
        out_shape=jax.ShapeDtypeStruct((r_max,), jnp.int32),
        scratch_shapes=[pltpu.SemaphoreType.DMA],
        compiler_params=pltpu.CompilerParams(dimension_semantics=("arbitrary",)),
        name="route_index",
    )(dest, jnp.zeros((r_max,), jnp.int32))


def _load_expert_weights(pairs, stage_ref, sem, rows):
    jobs = [(w, res, c) for w, res in pairs for c in range(w.shape[0] // rows)]

    def copy(i):
        w, _, c = jobs[i]
        return pltpu.make_async_copy(w.at[pl.ds(c * rows, rows), :], stage_ref.at[i % 2], sem.at[i % 2])

    copy(0).start()
    for i, (_, res, c) in enumerate(jobs):
        if i + 1 < len(jobs):
            copy(i + 1).start()
        copy(i).wait()
        res[c * rows:(c + 1) * rows, :] = stage_ref[i % 2].astype(BF16)


def _expert_changed(be_ref, b):
    return jnp.logical_or(b == 0, be_ref[b] != be_ref[jnp.maximum(b - 1, 0)])


def _expert_up_kernel(be_ref, nu_ref, src_ref, hp_hbm, w1_hbm, w3_hbm, o_ref,
                      xg_ref, xa_ref, xb_ref, w1_ref, w3_ref, stage_ref, gsem, wsem,
                      *, layer, stage_rows):
    b = pl.program_id(0)
    nu = nu_ref[0]
    bm = o_ref.shape[0]
    half = xg_ref.shape[2]

    def gather(blk, slot, wait, unroll):
        def body(r, c):
            cp = _row_copy(hp_hbm, src_ref[blk * bm + r], xg_ref.at[slot], r, gsem.at[slot])
            if wait:
                cp.wait()
            else:
                cp.start()
            return c

        lax.fori_loop(0, bm, body, 0, unroll=unroll)

    @pl.when(b < nu)
    def _():
        e = be_ref[b]
        slot = b % 2

        @pl.when(b == 0)
        def _():
            gather(0, 0, False, 8)

        @pl.when(_expert_changed(be_ref, b))
        def _():
            _load_expert_weights([(w1_hbm.at[layer, e], w1_ref), (w3_hbm.at[layer, e], w3_ref)],
                                 stage_ref, wsem, stage_rows)

        gather(b, slot, True, 8)
        xs = xg_ref[slot]
        xa_ref[...] = lax.bitcast_convert_type(xs & jnp.uint32(0xFFFF0000), F32).astype(BF16)
        xb_ref[...] = lax.bitcast_convert_type(xs << 16, F32).astype(BF16)
        nxt = jnp.minimum(b + 1, nu - 1)
        gather(nxt, 1 - slot, False, True)
        xa, xb = xa_ref[...], xb_ref[...]
        h1 = (jnp.dot(xa, w1_ref[0:half, :], preferred_element_type=F32)
              + jnp.dot(xb, w1_ref[half:2 * half, :], preferred_element_type=F32))
        h3 = (jnp.dot(xa, w3_ref[0:half, :], preferred_element_type=F32)
              + jnp.dot(xb, w3_ref[half:2 * half, :], preferred_element_type=F32))
        o_ref[...] = (h1 * jax.nn.sigmoid(h1) * h3).astype(BF16)

        @pl.when(b == nu - 1)
        def _():
            gather(nxt, 1 - slot, True, 8)

    @pl.when(b >= nu)
    def _():
        o_ref[...] = jnp.zeros(o_ref.shape, BF16)


def _expert_up(blk_exp, n_used, src, hp, w1, w3, layer, bm, stage_rows):
    r = src.shape[0]
    half = hp.shape[1]
    d, fdim = w1.shape[2], w1.shape[3]
    any_spec = pl.BlockSpec(memory_space=pl.ANY)
    return pl.pallas_call(
        functools.partial(_expert_up_kernel, layer=layer, stage_rows=stage_rows),
        grid_spec=pltpu.PrefetchScalarGridSpec(
            num_scalar_prefetch=3,
            grid=(r // bm,),
            in_specs=[any_spec, any_spec, any_spec],
            out_specs=pl.BlockSpec((bm, fdim), lambda b, be, nu, src: (b, 0)),
            scratch_shapes=[pltpu.VMEM((2, bm, half), jnp.uint32),
                            pltpu.VMEM((bm, half), BF16),
                            pltpu.VMEM((bm, half), BF16),
                            pltpu.VMEM((d, fdim), BF16),
                            pltpu.VMEM((d, fdim), BF16),
                            pltpu.VMEM((2, stage_rows, fdim), F32),
                            pltpu.SemaphoreType.DMA((2,)),
                            pltpu.SemaphoreType.DMA((2,))]),
        out_shape=jax.ShapeDtypeStruct((r, fdim), BF16),
        compiler_params=_params(("arbitrary",), 56),
        name="expert_up",
    )(blk_exp, n_used, src, hp, w1, w3)


def _expert_down_kernel(be_ref, nu_ref, h_ref, w2_hbm, o_ref, w2_ref, stage_ref, wsem,
                        *, layer, stage_rows):
    b = pl.program_id(0)

    @pl.when(b < nu_ref[0])
    def _():
        @pl.when(_expert_changed(be_ref, b))
        def _():
            _load_expert_weights([(w2_hbm.at[layer, be_ref[b]], w2_ref)], stage_ref, wsem, stage_rows)

        o_ref[...] = jnp.dot(h_ref[...], w2_ref[...], preferred_element_type=F32)

    @pl.when(b >= nu_ref[0])
    def _():
        o_ref[...] = jnp.zeros(o_ref.shape, F32)


def _expert_down(blk_exp, n_used, hmid, w2, layer, bm, stage_rows):
    r, fdim = hmid.shape
    d = w2.shape[3]
    return pl.pallas_call(
        functools.partial(_expert_down_kernel, layer=layer, stage_rows=stage_rows),
        grid_spec=pltpu.PrefetchScalarGridSpec(
            num_scalar_prefetch=2,
            grid=(r // bm,),
            in_specs=[pl.BlockSpec((bm, fdim), lambda b, be, nu: (jnp.maximum(jnp.minimum(b, nu[0] - 1), 0), 0)),
                      pl.BlockSpec(memory_space=pl.ANY)],
            out_specs=pl.BlockSpec((bm, d), lambda b, be, nu: (b, 0)),
            scratch_shapes=[pltpu.VMEM((fdim, d), BF16),
                            pltpu.VMEM((2, stage_rows, d), F32),
                            pltpu.SemaphoreType.DMA((2,))]),
        out_shape=jax.ShapeDtypeStruct((r, d), F32),
        compiler_params=_params(("arbitrary",), 48),
        name="expert_down",
    )(blk_exp, n_used, hmid, w2)


def _combine_kernel(dest_ref, gate_ref, x_ref, pa_ref, pb_ref, g_ref, wd_ref, wu_ref, wp_ref, y_hbm,
                    o_ref, ybuf_ref, sem, *, tiles_a):
    i = pl.program_id(0)
    n = pl.num_programs(0)
    tq = x_ref.shape[0]
    t_total = dest_ref.shape[0] // TOP_K
    slot = i % 2

    def gather(tile, slot_, wait, unroll):
        def body(t, c):
            for k in range(TOP_K):
                cp = _row_copy(y_hbm, dest_ref[k * t_total + tile * tq + t], ybuf_ref.at[slot_, k], t,
                               sem.at[slot_])
                if wait:
                    cp.wait()
                else:
                    cp.start()
            return c

        lax.fori_loop(0, tq, body, 0, unroll=unroll)

    @pl.when(i == 0)
    def _():
        gather(0, 0, False, 4)

    gather(i, slot, True, 4)
    moe = ybuf_ref[slot, 0] * gate_ref[:, 0:1] + ybuf_ref[slot, 1] * gate_ref[:, 1:2]
    x = x_ref[...] + moe
    nxt = jnp.minimum(i + 1, n - 1)
    gather(nxt, 1 - slot, False, True)
    hn = _rms(x, g_ref[...]).astype(BF16)
    low = jnp.dot(hn, wd_ref[...], preferred_element_type=F32).astype(BF16)
    gate = jax.nn.sigmoid(jnp.dot(low, wu_ref[...], preferred_element_type=F32))
    p = jnp.where(i < tiles_a, pa_ref[...], pb_ref[...])
    emb = jnp.dot(p.astype(BF16), wp_ref[...], preferred_element_type=F32)
    o_ref[...] = x + gate * emb

    @pl.when(i == n - 1)
    def _():
        gather(nxt, 1 - slot, True, 4)


def _combine(dest, gates, x, p_a, p_b, g, wd, wu, wp, ybuf, layer, tq):
    t, d = x.shape
    pd = p_a.shape[2]
    tiles_a, tiles_b = p_a.shape[1] // tq, p_b.shape[1] // tq
    assert tiles_a * tq == p_a.shape[1] and tiles_b * tq == p_b.shape[1] and (tiles_a + tiles_b) * tq == t

    def full(a):
        return pl.BlockSpec(a.shape, lambda i, dest: (0, 0))

    return pl.pallas_call(
        functools.partial(_combine_kernel, tiles_a=tiles_a),
        grid_spec=pltpu.PrefetchScalarGridSpec(
            num_scalar_prefetch=1,
            grid=(t // tq,),
            in_specs=[pl.BlockSpec((tq, TOP_K), lambda i, dest: (i, 0)),
                      pl.BlockSpec((tq, d), lambda i, dest: (i, 0)),
                      pl.BlockSpec((None, tq, pd), lambda i, dest: (layer, jnp.minimum(i, tiles_a - 1), 0)),
                      pl.BlockSpec((None, tq, pd), lambda i, dest: (layer, jnp.maximum(i - tiles_a, 0), 0)),
                      full(g), full(wd), full(wu), full(wp),
                      pl.BlockSpec(memory_space=pl.ANY)],
            out_specs=pl.BlockSpec((tq, d), lambda i, dest: (i, 0)),
            scratch_shapes=[pltpu.VMEM((2, TOP_K, tq, d), F32), pltpu.SemaphoreType.DMA((2,))]),
        out_shape=jax.ShapeDtypeStruct((t, d), F32),
        input_output_aliases={2: 0},
        compiler_params=_params(("arbitrary",), 52),
        name="combine",
    )(dest, gates, x, p_a, p_b, g, wd, wu, wp, ybuf)


def _final_kernel(x_ref, g_ref, o_ref):
    o_ref[...] = _rms(x_ref[...], g_ref[...])


def _final_norm(x, g, row0, rows, tm):
    d = x.shape[1]
    rb0 = row0 // tm
    return pl.pallas_call(
        _final_kernel,
        grid=(rows // tm,),
        in_specs=[pl.BlockSpec((tm, d), lambda i: (rb0 + i, 0)),
                  pl.BlockSpec((1, d), lambda i: (0, 0))],
        out_specs=pl.BlockSpec((tm, d), lambda i: (i, 0)),
        out_shape=jax.ShapeDtypeStruct((rows, d), F32),
        compiler_params=_params(("parallel",), 32),
        name="final_norm",
    )(x, g)


def _tile(n, pref):
    t = min(n, pref)
    while n % t:
        t //= 2
    return t


def kernel(x_prompt, x_sample, state_pool, p_prompt, p_sample, g_mix, w_in, w_pool, pool_scale, sgu_ln_g, sgu_ln_b, sgu_w, sgu_b, w_out, g_ffn, router_grp_w, router_grp_b, router_exp_w, router_exp_b, moe_w1, moe_w3, moe_w2, g_ple, ple_gate_down, ple_gate_up, ple_proj, g_final):
    depth = w_in.shape[0]
    nb_p, seq_p, d = x_prompt.shape
    nb_s, seq_s, _ = x_sample.shape
    dp = state_pool.shape[-1]
    t_p, t_s = nb_p * seq_p, nb_s * seq_s
    t = t_p + t_s
    n_past_s = min(POOL_CACHE, PAST_LEN)

    tm = _tile(t, 512)
    tn_in = _tile(dp, 1024)
    tn_out = _tile(d, 1024)
    ts_p = _tile(seq_p, 256)
    tq = _tile(t, 256)
    bm = _tile(t * TOP_K, 512)
    stage_up = _tile(d, 512)
    stage_down = _tile(moe_w2.shape[2], 128)
    r_max = t * TOP_K + N_EXPERTS * bm
    nb_max = r_max // bm

    x = jnp.concatenate([x_prompt.reshape(t_p, d), x_sample.reshape(t_s, d)], axis=0)
    p_a, p_b = p_prompt.reshape(depth, t_p, -1), p_sample.reshape(depth, t_s, -1)
    bf = lambda a: a.astype(BF16)
    n_log = MOE_GROUPS + N_EXPERTS
    wr = jnp.concatenate([router_grp_w, router_exp_w], axis=-1)
    wr = jnp.pad(jnp.swapaxes(wr, 1, 2), ((0, 0), (0, ROUTER_ROWS - n_log), (0, 0)))
    wr_hi = wr.astype(BF16)
    wr_lo = (wr - wr_hi.astype(F32)).astype(BF16)
    br = jnp.concatenate([router_grp_b, router_exp_b], axis=-1)
    br = jnp.broadcast_to(jnp.pad(br, ((0, 0), (0, ROUTER_ROWS - n_log)))[:, :, None],
                          (depth, ROUTER_ROWS, 128))
    bst = jnp.swapaxes(sgu_b, 1, 2)

    pool_p, pool_s, v_s = [], [], []
    for i in range(depth):
        row = lambda a: a[i][None, :]
        z = _inproj(x, row(g_mix), bf(w_in[i]), dp, tm, tn_in)
        mix_w = (bf(w_pool[i]), row(pool_scale), row(sgu_ln_g), row(sgu_ln_b), sgu_w[i], bst[i])
        m_p, np_i = _mixer(z, None, *mix_w, layer=i, row0=0, n_seq=nb_p, seq=seq_p, ts=ts_p,
                           n_past=0, emit_v=False)
        m_s, ns_i, vs_i = _mixer(z, state_pool, *mix_w, layer=i, row0=t_p, n_seq=nb_s, seq=seq_s,
                                 ts=seq_s, n_past=n_past_s, emit_v=True)
        pool_p.append(np_i)
        pool_s.append(ns_i)
        v_s.append(vs_i)
        x = _outproj(m_p, m_s, bf(w_out[i]), x, tm, tn_out)

        hp, ri, rf, cnt = _router(x, row(g_ffn), wr_hi[i], wr_lo[i], br[i], tm)
        counts = cnt[:, 0].astype(jnp.int32)
        padded = (counts + bm - 1) // bm * bm
        pend = jnp.cumsum(padded)
        pstart = pend - padded
        n_used = (pend[-1:] // bm).astype(jnp.int32)
        blk_first = jnp.arange(nb_max, dtype=jnp.int32) * bm
        blk_exp = jnp.minimum(jnp.sum((pend[None, :] <= blk_first[:, None]).astype(jnp.int32), axis=1),
                              N_EXPERTS - 1)
        eid, rank = ri[0:TOP_K], ri[TOP_K:2 * TOP_K]
        onehot = eid[:, :, None] == jnp.arange(N_EXPERTS, dtype=jnp.int32)
        dest = (jnp.sum(jnp.where(onehot, pstart, 0), axis=-1) + rank).reshape(-1)
        src = _route_index(dest, r_max, tm)
        gates = jnp.transpose(rf[0:TOP_K])

        hmid = _expert_up(blk_exp, n_used, src, hp, moe_w1, moe_w3, i, bm, stage_up)
        ybuf = _expert_down(blk_exp, n_used, hmid, moe_w2, i, bm, stage_down)
        x = _combine(dest, gates, x, p_a, p_b, row(g_ple), bf(ple_gate_down[i]), bf(ple_gate_up[i]),
                     bf(ple_proj[i]), ybuf, i, tq)

    g_fin = g_final[None, :]
    y_prompt = _final_norm(x, g_fin, 0, t_p, tq).reshape(nb_p, seq_p, d)
    y_sample = _final_norm(x, g_fin, t_p, t_s, tq).reshape(nb_s, seq_s, d)
    return (y_prompt, y_sample, jnp.stack(pool_p), jnp.stack(pool_s), jnp.stack(v_s))
```

```python
import functools

import numpy as np
import jax
import jax.numpy as jnp
from jax import lax
from jax.experimental import pallas as pl
from jax.experimental.pallas import tpu as pltpu

PAST_LEN = 2048
POOL_WINDOWS = (2, 4, 8, 16)
POOL_CACHE = max(POOL_WINDOWS) - 1
POOL_PAD = 32
POOL_HALO = 16
SGU_CHUNK = 128
SGU_HEADS = 8
MOE_GROUPS = 4
MOE_EXP_PER_GROUP = 4
N_EXPERTS = MOE_GROUPS * MOE_EXP_PER_GROUP
TOP_K = 2
EPS = 1e-6
ROUTER_ROWS = 32

F32 = jnp.float32
BF16 = jnp.bfloat16
MIB = 2 ** 20


def _params(semantics, vmem_mib):
    return pltpu.CompilerParams(dimension_semantics=semantics, vmem_limit_bytes=vmem_mib * MIB)


def _rms(x, g):
    return x * lax.rsqrt(jnp.mean(x * x, axis=-1, keepdims=True) + EPS) * g


def _gelu(x):
    return 0.5 * x * (1.0 + lax.erf(x * np.float32(np.sqrt(0.5))))


def _inproj_kernel(x_ref, g_ref, w_ref, z_ref, h_ref, *, n_raw):
    j = pl.program_id(1)

    @pl.when(j == 0)
    def _():
        h_ref[...] = _rms(x_ref[...], g_ref[...]).astype(BF16)

    acc = jnp.dot(h_ref[...], w_ref[...], preferred_element_type=F32)

    @pl.when(j < n_raw)
    def _():
        z_ref[...] = acc

    @pl.when(j >= n_raw)
    def _():
        z_ref[...] = _gelu(acc)


def _inproj(x, g, w, d_pool, tm, tn):
    t, d = x.shape
    n = w.shape[1]
    return pl.pallas_call(
        functools.partial(_inproj_kernel, n_raw=d_pool // tn),
        grid=(t // tm, n // tn),
        in_specs=[pl.BlockSpec((tm, d), lambda i, j: (i, 0)),
                  pl.BlockSpec((1, d), lambda i, j: (0, 0)),
                  pl.BlockSpec((d, tn), lambda i, j: (0, j))],
        out_specs=pl.BlockSpec((tm, tn), lambda i, j: (i, j)),
        out_shape=jax.ShapeDtypeStruct((t, n), F32),
        scratch_shapes=[pltpu.VMEM((tm, d), BF16)],
        compiler_params=_params(("parallel", "arbitrary"), 60),
        name="inproj",
    )(x, g, w)


def _mixer_kernel(*refs, ts, chunk, has_past, n_past, emit_v):
    it = iter(refs)
    za_ref, u_ref, v_ref = next(it), next(it), next(it)
    past_ref = next(it) if has_past else None
    wpool_ref, pscale_ref, lng_ref, lnb_ref, ws_ref, bst_ref = (next(it) for _ in range(6))
    m_ref, pool_ref = next(it), next(it)
    vout_ref = next(it) if emit_v else None
    ext_ref, lvl_a_ref, lvl_b_ref = next(it), next(it), next(it)

    s = pl.program_id(1)
    dp = za_ref.shape[1]
    gc = dp // len(POOL_WINDOWS)
    lo = POOL_PAD

    @pl.when(s == 0)
    def _():
        for ref in (lvl_a_ref, lvl_b_ref):
            ref[0:POOL_HALO, :] = jnp.zeros((POOL_HALO, gc), F32)
        if has_past:
            ext_ref[0:lo - POOL_CACHE, :] = jnp.zeros((lo - POOL_CACHE, dp), F32)
            ext_ref[lo - POOL_CACHE:lo, :] = past_ref[...]
        else:
            ext_ref[0:lo, :] = jnp.zeros((lo, dp), F32)

    za = za_ref[...]
    ext_ref[lo:lo + ts, :] = za
    pos = s * ts + lax.broadcasted_iota(jnp.int32, (ts, 1), 0)
    top = POOL_HALO
    for g, w in enumerate(POOL_WINDOWS):
        c0 = g * gc
        src, cols, k = ext_ref, slice(c0, c0 + gc), 1
        while 2 * k < w:
            dst = (lvl_a_ref, lvl_b_ref)[(k.bit_length() - 1) % 2]
            dst[top:lo + ts, :] = src[top:lo + ts, cols] + src[top - k:lo + ts - k, cols]
            src, cols, k = dst, slice(0, gc), 2 * k
        win = src[lo:lo + ts, cols] + src[lo - k:lo + ts - k, cols]
        zg = za[:, c0:c0 + gc]
        cnt = jnp.minimum(w, pos + 1 + n_past).astype(F32)
        dlt = win * (1.0 / cnt) - zg
        y = jnp.dot(dlt.astype(BF16), wpool_ref[g], preferred_element_type=F32)
        m_ref[:, c0:c0 + gc] = (y * pscale_ref[:, c0:c0 + gc]).astype(BF16)

    tail = za[ts - POOL_CACHE:ts, :]
    pool_ref[...] = tail
    ext_ref[lo - POOL_CACHE:lo, :] = tail

    v = v_ref[...]
    mu = jnp.mean(v, axis=-1, keepdims=True)
    vc = v - mu
    var = jnp.mean(vc * vc, axis=-1, keepdims=True)
    vln = vc * lax.rsqrt(var + EPS) * lng_ref[...] + lnb_ref[...]
    if emit_v:
        vout_ref[...] = vln
    vb = vln.astype(BF16)
    u = u_ref[...]
    hd = u.shape[1] // SGU_HEADS
    tril = (lax.broadcasted_iota(jnp.int32, (chunk, chunk), 0)
            >= lax.broadcasted_iota(jnp.int32, (chunk, chunk), 1))
    for h in range(SGU_HEADS):
        wm = jnp.where(tril, ws_ref[h, 0:chunk, 0:chunk], 0.0).astype(BF16)
        bias = bst_ref[0:chunk, h:h + 1]
        for c in range(ts // chunk):
            rows = slice(c * chunk, (c + 1) * chunk)
            cols = slice(h * hd, (h + 1) * hd)
            mixed = jnp.dot(wm, vb[rows, cols], preferred_element_type=F32) + bias
            m_ref[rows, dp + h * hd:dp + (h + 1) * hd] = (u[rows, cols] * mixed).astype(BF16)


def _mixer(z, past, wpool, pscale, lng, lnb, ws, bst, *, layer, row0, n_seq, seq, ts, n_past, emit_v):
    t = z.shape[0]
    dp = pscale.shape[1]
    ds = lng.shape[1]
    assert dp == ds and z.shape[1] == dp + 2 * ds
    d = dp + ds
    n_tiles = seq // ts
    chunk = min(seq, SGU_CHUNK)
    assert ts % chunk == 0 and row0 % ts == 0
    rb0 = row0 // ts
    has_past = past is not None

    def zspec(col):
        return pl.BlockSpec((ts, dp), lambda b, s: (rb0 + b * n_tiles + s, col))

    def full(a):
        nd = a.ndim
        return pl.BlockSpec(a.shape, lambda b, s: (0,) * nd)

    in_specs = [zspec(0), zspec(1), zspec(2)]
    args = [z, z, z]
    if has_past:
        in_specs.append(pl.BlockSpec((None, None, POOL_CACHE, dp), lambda b, s: (layer, b, 0, 0)))
        args.append(past)
    small = [wpool, pscale, lng, lnb, ws, bst]
    in_specs += [full(a) for a in small]
    args += small

    out_shape = [jax.ShapeDtypeStruct((n_seq * seq, d), BF16),
                 jax.ShapeDtypeStruct((n_seq, POOL_CACHE, dp), F32)]
    out_specs = [pl.BlockSpec((ts, d), lambda b, s: (b * n_tiles + s, 0)),
                 pl.BlockSpec((None, POOL_CACHE, dp), lambda b, s: (b, 0, 0))]
    if emit_v:
        out_shape.append(jax.ShapeDtypeStruct((n_seq, seq, ds), F32))
        out_specs.append(pl.BlockSpec((None, ts, ds), lambda b, s: (b, s, 0)))
    del t
    return pl.pallas_call(
        functools.partial(_mixer_kernel, ts=ts, chunk=chunk, has_past=has_past,
                          n_past=n_past, emit_v=emit_v),
        grid=(n_seq, n_tiles),
        in_specs=in_specs,
        out_specs=out_specs,
        out_shape=out_shape,
        scratch_shapes=[pltpu.VMEM((POOL_PAD + ts, dp), F32),
                        pltpu.VMEM((POOL_PAD + ts, dp // len(POOL_WINDOWS)), F32),
                        pltpu.VMEM((POOL_PAD + ts, dp // len(POOL_WINDOWS)), F32)],
        compiler_params=_params(("parallel", "arbitrary"), 48),
        name="mixer_v" if emit_v else "mixer",
    )(*args)


def _outproj_kernel(ma_ref, mb_ref, w_ref, x_ref, o_ref, *, tiles_a):
    i = pl.program_id(0)

    @pl.when(i < tiles_a)
    def _():
        o_ref[...] = x_ref[...] + jnp.dot(ma_ref[...], w_ref[...], preferred_element_type=F32)

    @pl.when(i >= tiles_a)
    def _():
        o_ref[...] = x_ref[...] + jnp.dot(mb_ref[...], w_ref[...], preferred_element_type=F32)


def _outproj(m_a, m_b, w, x, tm, tn):
    t, d = x.shape
    k = m_a.shape[1]
    tiles_a, tiles_b = m_a.shape[0] // tm, m_b.shape[0] // tm
    assert tiles_a * tm == m_a.shape[0] and tiles_b * tm == m_b.shape[0] and (tiles_a + tiles_b) * tm == t
    return pl.pallas_call(
        functools.partial(_outproj_kernel, tiles_a=tiles_a),
        grid=(t // tm, d // tn),
        in_specs=[pl.BlockSpec((tm, k), lambda i, j: (jnp.minimum(i, tiles_a - 1), 0)),
                  pl.BlockSpec((tm, k), lambda i, j: (jnp.maximum(i - tiles_a, 0), 0)),
                  pl.BlockSpec((k, tn), lambda i, j: (0, j)),
                  pl.BlockSpec((tm, tn), lambda i, j: (i, j))],
        out_specs=pl.BlockSpec((tm, tn), lambda i, j: (i, j)),
        out_shape=jax.ShapeDtypeStruct((t, d), F32),
        input_output_aliases={3: 0},
        compiler_params=_params(("parallel", "arbitrary"), 52),
        name="outproj",
    )(m_a, m_b, w, x)


def _first_max(vals):
    mx = vals[0]
    for v in vals[1:]:
        mx = jnp.maximum(mx, v)
    idx = jnp.full(mx.shape, len(vals) - 1, jnp.int32)
    for j in range(len(vals) - 2, -1, -1):
        idx = jnp.where(vals[j] == mx, j, idx)
    return mx, idx


def _router_kernel(x_ref, g_ref, whi_ref, wlo_ref, b_ref, h_ref, ri_ref, rf_ref, cnt_ref, carry_ref):
    i = pl.program_id(0)
    tm = x_ref.shape[0]

    @pl.when(i == 0)
    def _():
        carry_ref[...] = jnp.zeros(carry_ref.shape, F32)

    h = _rms(x_ref[...], g_ref[...])
    hb = h.astype(BF16)
    half = h.shape[1] // 2
    hi = lax.bitcast_convert_type(hb[:, :half].astype(F32), jnp.uint32)
    lo = lax.bitcast_convert_type(hb[:, half:].astype(F32), jnp.uint32)
    h_ref[...] = hi | (lo >> 16)
    h_lo = (h - hb.astype(F32)).astype(BF16)
    nt = (((1,), (1,)), ((), ()))
    logits = (lax.dot_general(whi_ref[...], hb, nt, preferred_element_type=F32)
              + lax.dot_general(whi_ref[...], h_lo, nt, preferred_element_type=F32)
              + lax.dot_general(wlo_ref[...], hb, nt, preferred_element_type=F32)) + b_ref[:, 0:1]
    row = [logits[j:j + 1, :] for j in range(MOE_GROUPS + N_EXPERTS)]
    lg = row[:MOE_GROUPS]
    gmax, grp = _first_max(lg)
    gsum = jnp.exp(lg[0] - gmax)
    for j in range(1, MOE_GROUPS):
        gsum = gsum + jnp.exp(lg[j] - gmax)
    pg = 1.0 / gsum
    ev = []
    for j in range(MOE_EXP_PER_GROUP):
        sel = row[MOE_GROUPS + (MOE_GROUPS - 1) * MOE_EXP_PER_GROUP + j]
        for gi in range(MOE_GROUPS - 2, -1, -1):
            sel = jnp.where(grp == gi, row[MOE_GROUPS + gi * MOE_EXP_PER_GROUP + j], sel)
        ev.append(sel)
    v1, i1 = _first_max(ev)
    rest = [jnp.where(i1 == j, -jnp.inf, ev[j]) for j in range(MOE_EXP_PER_GROUP)]
    v2, i2 = _first_max(rest)
    e21 = jnp.exp(v2 - v1)
    den = 1.0 + e21
    gate0 = pg * (1.0 / den)
    gate1 = pg * (e21 / den)
    eid0 = grp * MOE_EXP_PER_GROUP + i1
    eid1 = grp * MOE_EXP_PER_GROUP + i2

    erow = lax.broadcasted_iota(jnp.int32, (N_EXPERTS, tm), 0)
    c0 = erow == eid0
    c1 = erow == eid1
    hit = jnp.logical_or(c0, c1)
    upper = (lax.broadcasted_iota(jnp.int32, (tm, tm), 0)
             < lax.broadcasted_iota(jnp.int32, (tm, tm), 1))
    before = jnp.dot(hit.astype(BF16), upper.astype(BF16), preferred_element_type=F32)
    before = before + carry_ref[:, 0:1]
    r0 = jnp.sum(jnp.where(c0, before, 0.0), axis=0, keepdims=True)
    r1 = jnp.sum(jnp.where(c1, before, 0.0), axis=0, keepdims=True)
    carry_ref[...] = carry_ref[...] + jnp.sum(hit.astype(F32), axis=1, keepdims=True)
    cnt_ref[...] = carry_ref[...]

    ri_ref[...] = jnp.zeros(ri_ref.shape, jnp.int32)
    ri_ref[0:1, :] = eid0
    ri_ref[1:2, :] = eid1
    ri_ref[2:3, :] = r0.astype(jnp.int32)
    ri_ref[3:4, :] = r1.astype(jnp.int32)
    rf_ref[...] = jnp.zeros(rf_ref.shape, F32)
    rf_ref[0:1, :] = gate0
    rf_ref[1:2, :] = gate1


def _router(x, g, wr_hi, wr_lo, br, tm):
    t, d = x.shape
    return pl.pallas_call(
        _router_kernel,
        grid=(t // tm,),
        in_specs=[pl.BlockSpec((tm, d), lambda i: (i, 0)),
                  pl.BlockSpec((1, d), lambda i: (0, 0)),
                  pl.BlockSpec((ROUTER_ROWS, d), lambda i: (0, 0)),
                  pl.BlockSpec((ROUTER_ROWS, d), lambda i: (0, 0)),
                  pl.BlockSpec((ROUTER_ROWS, 128), lambda i: (0, 0))],
        out_specs=[pl.BlockSpec((tm, d // 2), lambda i: (i, 0)),
                   pl.BlockSpec((8, tm), lambda i: (0, i)),
                   pl.BlockSpec((8, tm), lambda i: (0, i)),
                   pl.BlockSpec((N_EXPERTS, 128), lambda i: (0, 0))],
        out_shape=[jax.ShapeDtypeStruct((t, d // 2), jnp.uint32),
                   jax.ShapeDtypeStruct((8, t), jnp.int32),
                   jax.ShapeDtypeStruct((8, t), F32),
                   jax.ShapeDtypeStruct((N_EXPERTS, 128), F32)],
        scratch_shapes=[pltpu.VMEM((N_EXPERTS, 128), F32)],
        compiler_params=_params(("arbitrary",), 56),
        name="router",
    )(x, g, wr_hi, wr_lo, br)


def _row_copy(src_hbm, src_row, dst, dst_row, sem):
    return pltpu.make_async_copy(src_hbm.at[pl.ds(src_row, 1), :], dst.at[pl.ds(dst_row, 1), :], sem)


def _route_index_kernel(dest_ref, zeros_hbm, src_ref, sem, *, place_rows):
    i = pl.program_id(0)
    t_total = dest_ref.shape[0] // TOP_K

    @pl.when(i == 0)
    def _():
        fill = pltpu.make_async_copy(zeros_hbm, src_ref, sem)
        fill.start()
        fill.wait()

    base = i * place_rows

    def place(j, c):
        t = base + j
        for k in range(TOP_K):
            src_ref[dest_ref[k * t_total + t]] = t
        return c

    lax.fori_loop(0, place_rows, place, 0, unroll=8)


def _route_index(dest, r_max, place_rows):
    t = dest.shape[0] // TOP_K
    smem = pl.BlockSpec(memory_space=pltpu.SMEM)
    return pl.pallas_call(
        functools.partial(_route_index_kernel, place_rows=place_rows),
        grid=(t // place_rows,),
        in_specs=[smem, pl.BlockSpec(memory_space=pl.ANY)],
        out_specs=smem,
        out_shape=jax.ShapeDtypeStruct((r_max,), jnp.int32),
        scratch_shapes=[pltpu.SemaphoreType.DMA],
        compiler_params=pltpu.CompilerParams(dimension_semantics=("arbitrary",)),
        name="route_index",
    )(dest, jnp.zeros((r_max,), jnp.int32))


def _load_expert_weights(pairs, stage_ref, sem, rows):
    jobs = [(w, res, c) for w, res in pairs for c in range(w.shape[0] // rows)]

    def copy(i):
        w, _, c = jobs[i]
        return pltpu.make_async_copy(w.at[pl.ds(c * rows, rows), :], stage_ref.at[i % 2], sem.at[i % 2])

    copy(0).start()
    for i, (_, res, c) in enumerate(jobs):
        if i + 1 < len(jobs):
            copy(i + 1).start()
        copy(i).wait()
        res[c * rows:(c + 1) * rows, :] = stage_ref[i % 2].astype(BF16)


def _expert_changed(be_ref, b):
    return jnp.logical_or(b == 0, be_ref[b] != be_ref[jnp.maximum(b - 1, 0)])


def _expert_up_kernel(be_ref, nu_ref, src_ref, hp_hbm, w1_hbm, w3_hbm, o_ref,
                      xg_ref, xa_ref, xb_ref, w1_ref, w3_ref, stage_ref, gsem, wsem,
                      *, layer, stage_rows):
    b = pl.program_id(0)
    nu = nu_ref[0]
    bm = o_ref.shape[0]
    half = xg_ref.shape[2]

    def gather(blk, slot, wait, unroll):
        def body(r, c):
            cp = _row_copy(hp_hbm, src_ref[blk * bm + r], xg_ref.at[slot], r, gsem.at[slot])
            if wait:
                cp.wait()
            else:
                cp.start()
            return c

        lax.fori_loop(0, bm, body, 0, unroll=unroll)

    @pl.when(b < nu)
    def _():
        e = be_ref[b]
        slot = b % 2

        @pl.when(b == 0)
        def _():
            gather(0, 0, False, 8)

        @pl.when(_expert_changed(be_ref, b))
        def _():
            _load_expert_weights([(w1_hbm.at[layer, e], w1_ref), (w3_hbm.at[layer, e], w3_ref)],
                                 stage_ref, wsem, stage_rows)

        gather(b, slot, True, 8)
        xs = xg_ref[slot]
        xa_ref[...] = lax.bitcast_convert_type(xs & jnp.uint32(0xFFFF0000), F32).astype(BF16)
        xb_ref[...] = lax.bitcast_convert_type(xs << 16, F32).astype(BF16)
        nxt = jnp.minimum(b + 1, nu - 1)
        gather(nxt, 1 - slot, False, True)
        xa, xb = xa_ref[...], xb_ref[...]
        h1 = (jnp.dot(xa, w1_ref[0:half, :], preferred_element_type=F32)
              + jnp.dot(xb, w1_ref[half:2 * half, :], preferred_element_type=F32))
        h3 = (jnp.dot(xa, w3_ref[0:half, :], preferred_element_type=F32)
              + jnp.dot(xb, w3_ref[half:2 * half, :], preferred_element_type=F32))
        o_ref[...] = (h1 * jax.nn.sigmoid(h1) * h3).astype(BF16)

        @pl.when(b == nu - 1)
        def _():
            gather(nxt, 1 - slot, True, 8)

    @pl.when(b >= nu)
    def _():
        o_ref[...] = jnp.zeros(o_ref.shape, BF16)


def _expert_up(blk_exp, n_used, src, hp, w1, w3, layer, bm, stage_rows):
    r = src.shape[0]
    half = hp.shape[1]
    d, fdim = w1.shape[2], w1.shape[3]
    any_spec = pl.BlockSpec(memory_space=pl.ANY)
    return pl.pallas_call(
        functools.partial(_expert_up_kernel, layer=layer, stage_rows=stage_rows),
        grid_spec=pltpu.PrefetchScalarGridSpec(
            num_scalar_prefetch=3,
            grid=(r // bm,),
            in_specs=[any_spec, any_spec, any_spec],
            out_specs=pl.BlockSpec((bm, fdim), lambda b, be, nu, src: (b, 0)),
            scratch_shapes=[pltpu.VMEM((2, bm, half), jnp.uint32),
                            pltpu.VMEM((bm, half), BF16),
                            pltpu.VMEM((bm, half), BF16),
                            pltpu.VMEM((d, fdim), BF16),
                            pltpu.VMEM((d, fdim), BF16),
                            pltpu.VMEM((2, stage_rows, fdim), F32),
                            pltpu.SemaphoreType.DMA((2,)),
                            pltpu.SemaphoreType.DMA((2,))]),
        out_shape=jax.ShapeDtypeStruct((r, fdim), BF16),
        compiler_params=_params(("arbitrary",), 56),
        name="expert_up",
    )(blk_exp, n_used, src, hp, w1, w3)


def _expert_down_kernel(be_ref, nu_ref, par_ref, first_ref, nxt_ref, h_ref, w2_hbm, o_ref,
                        w2_ref, stage_ref, wsem, *, layer, stage_rows, n_slices):
    b = pl.program_id(0)
    fdim, d = w2_ref.shape[1], w2_ref.shape[2]

    @pl.when(b < nu_ref[0])
    def _():
        cur = par_ref[b]

        @pl.when(b == 0)
        def _():
            _load_expert_weights([(w2_hbm.at[layer, be_ref[0]], w2_ref.at[cur])], stage_ref, wsem, stage_rows)

        @pl.when(first_ref[b] == 0)
        def _():
            o_ref[...] = jnp.dot(h_ref[...], w2_ref[cur], preferred_element_type=F32)

        @pl.when(first_ref[b] != 0)
        def _():
            w_next = w2_hbm.at[layer, nxt_ref[b]]
            res = w2_ref.at[1 - cur]
            n_jobs = fdim // stage_rows
            cw = d // n_slices

            def copy(q):
                return pltpu.make_async_copy(w_next.at[pl.ds(q * stage_rows, stage_rows), :],
                                             stage_ref.at[q % 2], wsem.at[q % 2])

            hm = h_ref[...]
            copy(0).start()
            for sl in range(n_slices):
                cols = slice(sl * cw, (sl + 1) * cw)
                o_ref[:, cols] = jnp.dot(hm, w2_ref[cur, :, cols], preferred_element_type=F32)
                for q in range(sl * n_jobs // n_slices, (sl + 1) * n_jobs // n_slices):
                    if q + 1 < n_jobs:
                        copy(q + 1).start()
                    copy(q).wait()
                    res[q * stage_rows:(q + 1) * stage_rows, :] = stage_ref[q % 2].astype(BF16)

    @pl.when(b >= nu_ref[0])
    def _():
        o_ref[...] = jnp.zeros(o_ref.shape, F32)


def _expert_down(blk_exp, n_used, hmid, w2, layer, bm, stage_rows):
    r, fdim = hmid.shape
    d = w2.shape[3]
    nb = blk_exp.shape[0]
    first = jnp.concatenate([jnp.ones((1,), jnp.int32), (blk_exp[1:] != blk_exp[:-1]).astype(jnp.int32)])
    par = (jnp.cumsum(first) - 1) % 2
    bigger = jnp.where(blk_exp[None, :] > blk_exp[:, None], blk_exp[None, :], N_EXPERTS)
    cand = jnp.min(bigger, axis=1)
    nxt = jnp.where(cand == N_EXPERTS, blk_exp, cand).astype(jnp.int32)
    del nb
    return pl.pallas_call(
        functools.partial(_expert_down_kernel, layer=layer, stage_rows=stage_rows, n_slices=4),
        grid_spec=pltpu.PrefetchScalarGridSpec(
            num_scalar_prefetch=5,
            grid=(r // bm,),
            in_specs=[pl.BlockSpec((bm, fdim),
                                   lambda b, be, nu, *_: (jnp.maximum(jnp.minimum(b, nu[0] - 1), 0), 0)),
                      pl.BlockSpec(memory_space=pl.ANY)],
            out_specs=pl.BlockSpec((bm, d), lambda b, *_: (b, 0)),
            scratch_shapes=[pltpu.VMEM((2, fdim, d), BF16),
                            pltpu.VMEM((2, stage_rows, d), F32),
                            pltpu.SemaphoreType.DMA((2,))]),
        out_shape=jax.ShapeDtypeStruct((r, d), F32),
        compiler_params=_params(("arbitrary",), 52),
        name="expert_down",
    )(blk_exp, n_used, par.astype(jnp.int32), first, nxt, hmid, w2)


def _combine_kernel(dest_ref, gate_ref, x_ref, pa_ref, pb_ref, g_ref, wd_ref, wu_ref, wp_ref, y_hbm,
                    o_ref, ybuf_ref, sem, *, tiles_a):
    i = pl.program_id(0)
    n = pl.num_programs(0)
    tq = x_ref.shape[0]
    t_total = dest_ref.shape[0] // TOP_K
    slot = i % 2

    def gather(tile, slot_, wait, unroll):
        def body(t, c):
            for k in range(TOP_K):
                cp = _row_copy(y_hbm, dest_ref[k * t_total + tile * tq + t], ybuf_ref.at[slot_, k], t,
                               sem.at[slot_])
                if wait:
                    cp.wait()
                else:
                    cp.start()
            return c

        lax.fori_loop(0, tq, body, 0, unroll=unroll)

    @pl.when(i == 0)
    def _():
        gather(0, 0, False, 4)

    gather(i, slot, True, 4)
    moe = ybuf_ref[slot, 0] * gate_ref[:, 0:1] + ybuf_ref[slot, 1] * gate_ref[:, 1:2]
    x = x_ref[...] + moe
    nxt = jnp.minimum(i + 1, n - 1)
    gather(nxt, 1 - slot, False, True)
    hn = _rms(x, g_ref[...]).astype(BF16)
    low = jnp.dot(hn, wd_ref[...], preferred_element_type=F32).astype(BF16)
    gate = jax.nn.sigmoid(jnp.dot(low, wu_ref[...], preferred_element_type=F32))
    p = jnp.where(i < tiles_a, pa_ref[...], pb_ref[...])
    emb = jnp.dot(p.astype(BF16), wp_ref[...], preferred_element_type=F32)
    o_ref[...] = x + gate * emb

    @pl.when(i == n - 1)
    def _():
        gather(nxt, 1 - slot, True, 4)


def _combine(dest, gates, x, p_a, p_b, g, wd, wu, wp, ybuf, layer, tq):
    t, d = x.shape
    pd = p_a.shape[2]
    tiles_a, tiles_b = p_a.shape[1] // tq, p_b.shape[1] // tq
    assert tiles_a * tq == p_a.shape[1] and tiles_b * tq == p_b.shape[1] and (tiles_a + tiles_b) * tq == t

    def full(a):
        return pl.BlockSpec(a.shape, lambda i, dest: (0, 0))

    return pl.pallas_call(
        functools.partial(_combine_kernel, tiles_a=tiles_a),
        grid_spec=pltpu.PrefetchScalarGridSpec(
            num_scalar_prefetch=1,
            grid=(t // tq,),
            in_specs=[pl.BlockSpec((tq, TOP_K), lambda i, dest: (i, 0)),
                      pl.BlockSpec((tq, d), lambda i, dest: (i, 0)),
                      pl.BlockSpec((None, tq, pd), lambda i, dest: (layer, jnp.minimum(i, tiles_a - 1), 0)),
                      pl.BlockSpec((None, tq, pd), lambda i, dest: (layer, jnp.maximum(i - tiles_a, 0), 0)),
                      full(g), full(wd), full(wu), full(wp),
                      pl.BlockSpec(memory_space=pl.ANY)],
            out_specs=pl.BlockSpec((tq, d), lambda i, dest: (i, 0)),
            scratch_shapes=[pltpu.VMEM((2, TOP_K, tq, d), F32), pltpu.SemaphoreType.DMA((2,))]),
        out_shape=jax.ShapeDtypeStruct((t, d), F32),
        input_output_aliases={2: 0},
        compiler_params=_params(("arbitrary",), 52),
        name="combine",
    )(dest, gates, x, p_a, p_b, g, wd, wu, wp, ybuf)


def _final_kernel(x_ref, g_ref, o_ref):
    o_ref[...] = _rms(x_ref[...], g_ref[...])


def _final_norm(x, g, row0, rows, tm):
    d = x.shape[1]
    rb0 = row0 // tm
    return pl.pallas_call(
        _final_kernel,
        grid=(rows // tm,),
        in_specs=[pl.BlockSpec((tm, d), lambda i: (rb0 + i, 0)),
                  pl.BlockSpec((1, d), lambda i: (0, 0))],
        out_specs=pl.BlockSpec((tm, d), lambda i: (i, 0)),
        out_shape=jax.ShapeDtypeStruct((rows, d), F32),
        compiler_params=_params(("parallel",), 32),
        name="final_norm",
    )(x, g)


def _tile(n, pref):
    t = min(n, pref)
    while n % t:
        t //= 2
    return t


def kernel(x_prompt, x_sample, state_pool, p_prompt, p_sample, g_mix, w_in, w_pool, pool_scale, sgu_ln_g, sgu_ln_b, sgu_w, sgu_b, w_out, g_ffn, router_grp_w, router_grp_b, router_exp_w, router_exp_b, moe_w1, moe_w3, moe_w2, g_ple, ple_gate_down, ple_gate_up, ple_proj, g_final):
    depth = w_in.shape[0]
    nb_p, seq_p, d = x_prompt.shape
    nb_s, seq_s, _ = x_sample.shape
    dp = state_pool.shape[-1]
    t_p, t_s = nb_p * seq_p, nb_s * seq_s
    t = t_p + t_s
    n_past_s = min(POOL_CACHE, PAST_LEN)

    tm = _tile(t, 512)
    tn_in = _tile(dp, 1024)
    tn_out = _tile(d, 1024)
    ts_p = _tile(seq_p, 256)
    tq = _tile(t, 256)
    bm = _tile(t * TOP_K, 512)
    stage_up = _tile(d, 512)
    stage_down = _tile(moe_w2.shape[2], 128)
    r_max = t * TOP_K + N_EXPERTS * bm
    nb_max = r_max // bm

    x = jnp.concatenate([x_prompt.reshape(t_p, d), x_sample.reshape(t_s, d)], axis=0)
    p_a, p_b = p_prompt.reshape(depth, t_p, -1), p_sample.reshape(depth, t_s, -1)
    bf = lambda a: a.astype(BF16)
    n_log = MOE_GROUPS + N_EXPERTS
    wr = jnp.concatenate([router_grp_w, router_exp_w], axis=-1)
    wr = jnp.pad(jnp.swapaxes(wr, 1, 2), ((0, 0), (0, ROUTER_ROWS - n_log), (0, 0)))
    wr_hi = wr.astype(BF16)
    wr_lo = (wr - wr_hi.astype(F32)).astype(BF16)
    br = jnp.concatenate([router_grp_b, router_exp_b], axis=-1)
    br = jnp.broadcast_to(jnp.pad(br, ((0, 0), (0, ROUTER_ROWS - n_log)))[:, :, None],
                          (depth, ROUTER_ROWS, 128))
    bst = jnp.swapaxes(sgu_b, 1, 2)

    pool_p, pool_s, v_s = [], [], []
    for i in range(depth):
        row = lambda a: a[i][None, :]
        z = _inproj(x, row(g_mix), bf(w_in[i]), dp, tm, tn_in)
        mix_w = (bf(w_pool[i]), row(pool_scale), row(sgu_ln_g), row(sgu_ln_b), sgu_w[i], bst[i])
        m_p, np_i = _mixer(z, None, *mix_w, layer=i, row0=0, n_seq=nb_p, seq=seq_p, ts=ts_p,
                           n_past=0, emit_v=False)
        m_s, ns_i, vs_i = _mixer(z, state_pool, *mix_w, layer=i, row0=t_p, n_seq=nb_s, seq=seq_s,
                                 ts=seq_s, n_past=n_past_s, emit_v=True)
        pool_p.append(np_i)
        pool_s.append(ns_i)
        v_s.append(vs_i)
        x = _outproj(m_p, m_s, bf(w_out[i]), x, tm, tn_out)

        hp, ri, rf, cnt = _router(x, row(g_ffn), wr_hi[i], wr_lo[i], br[i], tm)
        counts = cnt[:, 0].astype(jnp.int32)
        padded = (counts + bm - 1) // bm * bm
        pend = jnp.cumsum(padded)
        pstart = pend - padded
        n_used = (pend[-1:] // bm).astype(jnp.int32)
        blk_first = jnp.arange(nb_max, dtype=jnp.int32) * bm
        blk_exp = jnp.minimum(jnp.sum((pend[None, :] <= blk_first[:, None]).astype(jnp.int32), axis=1),
                              N_EXPERTS - 1)
        eid, rank = ri[0:TOP_K], ri[TOP_K:2 * TOP_K]
        onehot = eid[:, :, None] == jnp.arange(N_EXPERTS, dtype=jnp.int32)
        dest = (jnp.sum(jnp.where(onehot, pstart, 0), axis=-1) + rank).reshape(-1)
        src = _route_index(dest, r_max, tm)
        gates = jnp.transpose(rf[0:TOP_K])

        hmid = _expert_up(blk_exp, n_used, src, hp, moe_w1, moe_w3, i, bm, stage_up)
        ybuf = _expert_down(blk_exp, n_used, hmid, moe_w2, i, bm, stage_down)
        x = _combine(dest, gates, x, p_a, p_b, row(g_ple), bf(ple_gate_down[i]), bf(ple_gate_up[i]),
                     bf(ple_proj[i]), ybuf, i, tq)

    g_fin = g_final[None, :]
    y_prompt = _final_norm(x, g_fin, 0, t_p, tq).reshape(nb_p, seq_p, d)
    y_sample = _final_norm(x, g_fin, t_p, t_s, tq).reshape(nb_s, seq_s, d)
    return (y_prompt, y_sample, jnp.stack(pool_p), jnp.stack(pool_s), jnp.stack(v_s))
```
